```python
import jax, jax.numpy as jnp
from jax import lax
import numpy as np

D_MODEL = 1024
BATCH = 8
SEQ = 4096
DEPTH = 2

GRID_W = 64
CTX_LEN = 256
N_MIXERS = 2
EPS = 1e-6

SSD_EXPAND = 2
D_INNER = SSD_EXPAND * D_MODEL
SSD_HEADDIM = 64
SSD_HEADS = D_INNER // SSD_HEADDIM
SSD_GROUPS = 8
SSD_HPG = SSD_HEADS // SSD_GROUPS
SSD_STATE = 128
SSD_CONV_K = 5
SSD_CHUNK = 64
SSD_GN = SSD_GROUPS * SSD_STATE
SSD_CONV_DIM = D_INNER + 2 * SSD_GN
SSD_PROJ_DIM = D_INNER + SSD_CONV_DIM + 2 * SSD_HEADS

CONF_K = 31
CONF_DIM = D_MODEL
CONF_H = CONF_DIM // 2

FFN_HIDDEN = (((8 * D_MODEL + 2) // 3 + 255) // 256) * 256

N_SSD_LAYERS = (DEPTH + 1) // 2
N_CONF_LAYERS = DEPTH // 2

kernel_name = "hybrid_ssd_conformer_dit_trunk"


def rmsnorm(x, g):
    xf = x.astype(jnp.float32)
    y = xf * lax.rsqrt(jnp.mean(xf * xf, axis=-1, keepdims=True) + EPS)
    return (y * g.astype(jnp.float32)).astype(x.dtype)


def layernorm(x, g, b):
    xf = x.astype(jnp.float32)
    mu = jnp.mean(xf, axis=-1, keepdims=True)
    var = jnp.mean(jnp.square(xf - mu), axis=-1, keepdims=True)
    y = (xf - mu) * lax.rsqrt(var + EPS) * g.astype(jnp.float32) + b.astype(jnp.float32)
    return y.astype(x.dtype)


def modulate(h, shift, scale):
    return h * (1 + scale) + shift


def ada_params(sc, w, b):
    mod = sc @ w + b
    return [t[:, None, :] for t in jnp.split(mod, 6, axis=-1)]


def dwconv1d(x, w, b):
    k = w.shape[0]
    y = lax.conv_general_dilated(
        x, w[:, None, :].astype(x.dtype), window_strides=(1,),
        padding=[(k // 2, k // 2)], dimension_numbers=("NWC", "WIO", "NWC"),
        feature_group_count=x.shape[-1])
    return y + b


def axial_dwconv(u, rows, w, b):
    n, l, ch = u.shape
    g = u.reshape(n, rows, GRID_W, ch)
    hor = dwconv1d(g[..., :CONF_H].reshape(n * rows, GRID_W, CONF_H), w[:, :CONF_H], b[:CONF_H])
    hor = hor.reshape(n, rows, GRID_W, CONF_H)
    cv = ch - CONF_H
    ver_in = jnp.swapaxes(g[..., CONF_H:], 1, 2).reshape(n * GRID_W, rows, cv)
    ver = dwconv1d(ver_in, w[:, CONF_H:], b[CONF_H:]).reshape(n, GRID_W, rows, cv)
    ver = jnp.swapaxes(ver, 1, 2)
    return jnp.concatenate([hor, ver], axis=-1).reshape(n, l, ch)


def ssd_chunked(xh, dt, A, Bm, Cm, h0):
    n, l = xh.shape[:2]
    nc, q = l // SSD_CHUNK, SSD_CHUNK
    x = (xh * dt[..., None]).reshape(n, nc, q, SSD_GROUPS, SSD_HPG, SSD_HEADDIM)
    a_cum = jnp.cumsum((dt * A).reshape(n, nc, q, SSD_GROUPS, SSD_HPG), axis=2)
    Bc = Bm.reshape(n, nc, q, SSD_GROUPS, SSD_STATE)
    Cc = Cm.reshape(n, nc, q, SSD_GROUPS, SSD_STATE)
    idx = jnp.arange(q)
    lower = (idx[:, None] >= idx[None, :])[None, None, :, :, None, None]
    seg = a_cum[:, :, :, None] - a_cum[:, :, None, :]
    decay = jnp.exp(jnp.where(lower, seg, -jnp.inf))
    scores = jnp.einsum("bcqgn,bcsgn->bcqsg", Cc, Bc)
    y_diag = jnp.einsum("bcqsg,bcqsgj,bcsgjp->bcqgjp", scores, decay, x)
    decay_to_end = jnp.exp(a_cum[:, :, -1:] - a_cum)
    states = jnp.einsum("bcsgn,bcsgj,bcsgjp->bcgjpn", Bc, decay_to_end, x)
    chunk_decay = jnp.exp(a_cum[:, :, -1])

    def step(h, inp):
        st, dec = inp
        return h * dec[..., None, None] + st, h

    h_init = h0.reshape(n, SSD_GROUPS, SSD_HPG, SSD_HEADDIM, SSD_STATE)
    h_final, h_prev = lax.scan(step, h_init, (jnp.moveaxis(states, 1, 0), jnp.moveaxis(chunk_decay, 1, 0)))
    h_prev = jnp.moveaxis(h_prev, 0, 1)
    y_off = jnp.einsum("bcqgn,bcgjpn,bcqgj->bcqgjp", Cc, h_prev, jnp.exp(a_cum))
    y = (y_diag + y_off).reshape(n, l, SSD_HEADS, SSD_HEADDIM)
    return y, h_final.reshape(n, SSD_HEADS, SSD_HEADDIM, SSD_STATE)


def ssd_mixer(h, h0_f, h0_b, w_in, conv_w, conv_b, dt_bias_f, dt_bias_b, a_log_f, a_log_b,
              d_skip, norm_w, w_out):
    n, l, _ = h.shape
    f32 = jnp.float32
    zxbcdt = h @ w_in
    z = zxbcdt[..., :D_INNER].astype(f32)
    xbc = jax.nn.silu(dwconv1d(zxbcdt[..., D_INNER:D_INNER + SSD_CONV_DIM], conv_w, conv_b)).astype(f32)
    dt = zxbcdt[..., D_INNER + SSD_CONV_DIM:].astype(f32)
    xs = xbc[..., :D_INNER].reshape(n, l, SSD_HEADS, SSD_HEADDIM)
    Bm = xbc[..., D_INNER:D_INNER + SSD_GN].reshape(n, l, SSD_GROUPS, SSD_STATE)
    Cm = xbc[..., D_INNER + SSD_GN:].reshape(n, l, SSD_GROUPS, SSD_STATE)
    dt_f = jax.nn.softplus(dt[..., :SSD_HEADS] + dt_bias_f.astype(f32))
    dt_b = jax.nn.softplus(dt[..., SSD_HEADS:] + dt_bias_b.astype(f32))
    A_f = -jnp.exp(a_log_f.astype(f32))
    A_b = -jnp.exp(a_log_b.astype(f32))
    y_f, hf = ssd_chunked(xs, dt_f, A_f, Bm, Cm, h0_f)
    flip = lambda t: jnp.flip(t, axis=1)
    y_b, hb = ssd_chunked(flip(xs), flip(dt_b), A_b, flip(Bm), flip(Cm), h0_b)
    y = y_f + flip(y_b) + d_skip.astype(f32)[:, None] * xs
    y = rmsnorm(y.reshape(n, l, D_INNER) * jax.nn.silu(z), norm_w)
    return y.astype(h.dtype) @ w_out, hf, hb


def conformer_conv_module(h, rows, w_pw1, b_pw1, dw_w, dw_b, ln_g, ln_b, w_pw2, b_pw2):
    u = h @ w_pw1 + b_pw1
    u = u[..., :CONF_DIM] * jax.nn.sigmoid(u[..., CONF_DIM:])
    v = dwconv1d(u, dw_w, dw_b) if rows is None else axial_dwconv(u, rows, dw_w, dw_b)
    v = jax.nn.silu(layernorm(v, ln_g, ln_b))
    return v @ w_pw2 + b_pw2


def swiglu(h, w_in, w_out):
    u = h @ w_in
    return (jax.nn.silu(u[..., :FFN_HIDDEN]) * u[..., FFN_HIDDEN:]) @ w_out


def setup_inputs(seed: int = 0) -> dict:
    key = jax.random.key(seed)
    ks = jax.random.split(key, 40)
    f32 = jnp.float32
    nrm = lambda k, shape, s: jax.random.normal(k, shape, f32) * s
    NS, NC, D = N_SSD_LAYERS, N_CONF_LAYERS, D_MODEL
    dt0 = jnp.exp(jax.random.uniform(ks[0], (2, NS, SSD_HEADS), f32,
                                     float(np.log(1e-3)), float(np.log(1e-1))))
    dt_bias = dt0 + jnp.log(-jnp.expm1(-dt0))
    a_log = jnp.log(jax.random.uniform(ks[1], (2, NS, SSD_HEADS), f32, 1.0, 16.0))
    return {
        "x": nrm(ks[2], (BATCH, SEQ, D), 1.0),
        "c": nrm(ks[3], (BATCH, D), 1.0),
        "ctx": nrm(ks[4], (BATCH, CTX_LEN, D), 1.0),
        "c_ctx": nrm(ks[5], (D,), 1.0),
        "ada_w": nrm(ks[6], (DEPTH, D, 6 * D), 0.5 * D ** -0.5),
        "ada_b": nrm(ks[7], (DEPTH, 6 * D), 0.01),
        "norm_mix_g": 1.0 + nrm(ks[8], (DEPTH, D), 0.02),
        "norm_ffn_g": 1.0 + nrm(ks[9], (DEPTH, D), 0.02),
        "final_norm_g": 1.0 + nrm(ks[10], (D,), 0.02),
        "ssd_w_in": nrm(ks[11], (NS, D, SSD_PROJ_DIM), D ** -0.5),
        "ssd_conv_w": nrm(ks[12], (NS, SSD_CONV_K, SSD_CONV_DIM), SSD_CONV_K ** -0.5),
        "ssd_conv_b": nrm(ks[13], (NS, SSD_CONV_DIM), 0.01),
        "ssd_dt_bias_f": dt_bias[0],
        "ssd_dt_bias_b": dt_bias[1],
        "ssd_a_log_f": a_log[0],
        "ssd_a_log_b": a_log[1],
        "ssd_d_skip": 1.0 + nrm(ks[14], (NS, SSD_HEADS), 0.02),
        "ssd_norm_w": 1.0 + nrm(ks[15], (NS, D_INNER), 0.02),
        "ssd_w_out": nrm(ks[16], (NS, D_INNER, D), D_INNER ** -0.5),
        "conf_w_pw1": nrm(ks[17], (NC, D, 2 * CONF_DIM), D ** -0.5),
        "conf_b_pw1": nrm(ks[18], (NC, 2 * CONF_DIM), 0.01),
        "conf_dw_w": nrm(ks[19], (NC, CONF_K, CONF_DIM), CONF_K ** -0.5),
        "conf_dw_b": nrm(ks[20], (NC, CONF_DIM), 0.01),
        "conf_ln_g": 1.0 + nrm(ks[21], (NC, CONF_DIM), 0.02),
        "conf_ln_b": nrm(ks[22], (NC, CONF_DIM), 0.01),
        "conf_w_pw2": nrm(ks[23], (NC, CONF_DIM, D), CONF_DIM ** -0.5),
        "conf_b_pw2": nrm(ks[24], (NC, D), 0.01),
        "ffn_w_in": nrm(ks[25], (DEPTH, D, 2 * FFN_HIDDEN), D ** -0.5),
        "ffn_w_out": nrm(ks[26], (DEPTH, FFN_HIDDEN, D), FFN_HIDDEN ** -0.5),
    }


def reference(x, c, ctx, c_ctx, ada_w, ada_b, norm_mix_g, norm_ffn_g, final_norm_g,
              ssd_w_in, ssd_conv_w, ssd_conv_b, ssd_dt_bias_f, ssd_dt_bias_b, ssd_a_log_f,
              ssd_a_log_b, ssd_d_skip, ssd_norm_w, ssd_w_out,
              conf_w_pw1, conf_b_pw1, conf_dw_w, conf_dw_b, conf_ln_g, conf_ln_b,
              conf_w_pw2, conf_b_pw2, ffn_w_in, ffn_w_out):
    n, l, _ = x.shape
    rows = l // GRID_W
    h_lat, h_ctx = x, ctx
    sc_lat = jax.nn.silu(c)
    sc_ctx = jax.nn.silu(c_ctx)[None, :]
    for i in range(DEPTH):
        need_ctx_out = i < DEPTH - 1
        j = i // N_MIXERS
        sh1, s1, g1, sh2, s2, g2 = ada_params(sc_lat, ada_w[i], ada_b[i])
        xn_lat = modulate(rmsnorm(h_lat, norm_mix_g[i]), sh1, s1)
        if i % N_MIXERS == 0:
            p = (ssd_w_in[j], ssd_conv_w[j], ssd_conv_b[j], ssd_dt_bias_f[j], ssd_dt_bias_b[j],
                 ssd_a_log_f[j], ssd_a_log_b[j], ssd_d_skip[j], ssd_norm_w[j], ssd_w_out[j])
            csh1, cs1, cg1, csh2, cs2, cg2 = ada_params(sc_ctx, ada_w[i], ada_b[i])
            xn_ctx = modulate(rmsnorm(h_ctx, norm_mix_g[i]), csh1, cs1)
            h_zero = jnp.zeros((n, SSD_HEADS, SSD_HEADDIM, SSD_STATE), jnp.float32)
            y_ctx, hf_ctx, hb_ctx = ssd_mixer(xn_ctx, h_zero, h_zero, *p)
            y_lat, _, _ = ssd_mixer(xn_lat, hf_ctx, hb_ctx, *p)
            if need_ctx_out:
                h_ctx = h_ctx + cg1 * y_ctx
        else:
            p = (conf_w_pw1[j], conf_b_pw1[j], conf_dw_w[j], conf_dw_b[j], conf_ln_g[j],
                 conf_ln_b[j], conf_w_pw2[j], conf_b_pw2[j])
            y_lat = conformer_conv_module(xn_lat, rows, *p)
            if need_ctx_out:
                csh1, cs1, cg1, csh2, cs2, cg2 = ada_params(sc_ctx, ada_w[i], ada_b[i])
                xn_ctx = modulate(rmsnorm(h_ctx, norm_mix_g[i]), csh1, cs1)
                h_ctx = h_ctx + cg1 * conformer_conv_module(xn_ctx, None, *p)
        h_lat = h_lat + g1 * y_lat
        h_lat = h_lat + g2 * swiglu(modulate(rmsnorm(h_lat, norm_ffn_g[i]), sh2, s2), ffn_w_in[i], ffn_w_out[i])
        if need_ctx_out:
            h_ctx = h_ctx + cg2 * swiglu(modulate(rmsnorm(h_ctx, norm_ffn_g[i]), csh2, cs2),
                                         ffn_w_in[i], ffn_w_out[i])
    return rmsnorm(h_lat, final_norm_g)
```

```python
import functools

import jax
import jax.numpy as jnp
from jax import lax
from jax.experimental import pallas as pl
from jax.experimental.pallas import tpu as pltpu

F32 = jnp.float32
BF16 = jnp.bfloat16

EPS = 1e-6
D_MODEL = 1024
GRID_W = 64
D_INNER = 2048
HEADDIM = 64
HEADS = 32
GROUPS = 8
HPG = 4
STATE = 128
GN = GROUPS * STATE
CONV_K = 5
CONV_DIM = D_INNER + 2 * GN
DT_LANES = 128
SCAN_Q = 128
CONF_K = 31
CONF_H = 512
CONF_ROWS = 16
FFN_HIDDEN = 2816
FFN_CHUNK = 256

VMEM_LIMIT = 56 * 1024 * 1024


def _const_spec(shape):
    nd = len(shape)
    return pl.BlockSpec(shape, lambda *_: (0,) * nd, pipeline_mode=pl.Buffered(1))


def _params(sem):
    return pltpu.CompilerParams(dimension_semantics=sem, vmem_limit_bytes=VMEM_LIMIT)


def _silu(x):
    return x * jax.nn.sigmoid(x)


def _softplus(x):
    return jnp.maximum(x, 0.0) + jnp.log1p(jnp.exp(-jnp.abs(x)))


def _norm_modulate(x, g, shift, scale):
    y = x * lax.rsqrt(jnp.mean(x * x, axis=-1, keepdims=True) + EPS) * g
    return y * (1.0 + scale) + shift


def _ada_kernel(c_ref, w_ref, b_ref, o_ref):
    s = _silu(c_ref[...]).astype(BF16)
    o_ref[...] = jnp.dot(s, w_ref[...].astype(BF16), preferred_element_type=F32) + b_ref[...]


def _ada_call(cc, ada_w, ada_b):
    depth, d, n6 = ada_w.shape
    tn = 1536
    rows = cc.shape[0]
    return pl.pallas_call(
        _ada_kernel,
        grid=(depth, n6 // tn),
        in_specs=[
            pl.BlockSpec((rows, d), lambda i, j: (0, 0)),
            pl.BlockSpec((None, d, tn), lambda i, j: (i, 0, j)),
            pl.BlockSpec((None, 1, tn), lambda i, j: (i, 0, j)),
        ],
        out_specs=pl.BlockSpec((None, rows, tn), lambda i, j: (i, 0, j)),
        out_shape=jax.ShapeDtypeStruct((depth, rows, n6), F32),
        compiler_params=_params(("arbitrary", "arbitrary")),
        name="ada_params",
    )(cc, ada_w, ada_b.reshape(depth, 1, n6))


def _inproj_kernel(x_ref, sh_ref, sc_ref, g_ref, wz_ref, wx_ref, wd_ref, z_ref, xbc_ref, dt_ref):
    xm = _norm_modulate(x_ref[0], g_ref[...], sh_ref[0], sc_ref[0]).astype(BF16)
    nb = 512
    for j in range(D_INNER // nb):
        z_ref[0, :, j * nb:(j + 1) * nb] = jnp.dot(
            xm, wz_ref[:, j * nb:(j + 1) * nb], preferred_element_type=F32).astype(BF16)
    for j in range(CONV_DIM // nb):
        xbc_ref[0, :, j * nb:(j + 1) * nb] = jnp.dot(
            xm, wx_ref[:, j * nb:(j + 1) * nb], preferred_element_type=F32).astype(BF16)
    dt_ref[0] = jnp.dot(xm, wd_ref[...], preferred_element_type=F32)


def _inproj_call(h, shift, scale, g, wz, wx, wd, tm):
    n, l, d = h.shape
    row = lambda b, t: (b, t, 0)
    mod = lambda b, t: (b, 0, 0)
    return pl.pallas_call(
        _inproj_kernel,
        grid=(n, l // tm),
        in_specs=[
            pl.BlockSpec((1, tm, d), row),
            pl.BlockSpec((1, 1, d), mod),
            pl.BlockSpec((1, 1, d), mod),
            _const_spec((1, d)),
            _const_spec(wz.shape),
            _const_spec(wx.shape),
            _const_spec(wd.shape),
        ],
        out_specs=[
            pl.BlockSpec((1, tm, D_INNER), row),
            pl.BlockSpec((1, tm, CONV_DIM), row),
            pl.BlockSpec((1, tm, DT_LANES), row),
        ],
        out_shape=[
            jax.ShapeDtypeStruct((n, l, D_INNER), BF16),
            jax.ShapeDtypeStruct((n, l, CONV_DIM), BF16),
            jax.ShapeDtypeStruct((n, l, DT_LANES), F32),
        ],
        compiler_params=_params(("parallel", "parallel")),
        name="ssd_inproj",
    )(h, shift, scale, g, wz, wx, wd)


_CONV_PAD = 8


def _ssd_conv_kernel(x_ref, w_ref, b_ref, o_ref, pad_ref, *, l, rc):
    cb = x_ref.shape[-1]
    pad_ref[0:_CONV_PAD, :] = jnp.zeros((_CONV_PAD, cb), F32)
    pad_ref[_CONV_PAD + l:, :] = jnp.zeros((_CONV_PAD, cb), F32)
    for r in range(l // rc):
        pad_ref[_CONV_PAD + r * rc:_CONV_PAD + (r + 1) * rc, :] = x_ref[0, r * rc:(r + 1) * rc, :].astype(F32)
    base = _CONV_PAD - CONV_K // 2
    for r in range(l // rc):
        acc = jnp.broadcast_to(b_ref[...], (rc, cb))
        for k in range(CONV_K):
            s = base + k + r * rc
            acc = acc + pad_ref[s:s + rc, :] * w_ref[k:k + 1, :]
        o_ref[0, r * rc:(r + 1) * rc, :] = _silu(acc).astype(BF16)


def _ssd_conv_call(xbc, conv_w, conv_b):
    n, l, c = xbc.shape
    cb = 512
    return pl.pallas_call(
        functools.partial(_ssd_conv_kernel, l=l, rc=128),
        grid=(n, c // cb),
        in_specs=[
            pl.BlockSpec((1, l, cb), lambda b, j: (b, 0, j)),
            pl.BlockSpec((CONV_K, cb), lambda b, j: (0, j)),
            pl.BlockSpec((1, cb), lambda b, j: (0, j)),
        ],
        out_specs=pl.BlockSpec((1, l, cb), lambda b, j: (b, 0, j)),
        out_shape=jax.ShapeDtypeStruct((n, l, c), BF16),
        scratch_shapes=[pltpu.VMEM((l + 2 * _CONV_PAD, cb), F32)],
        compiler_params=_params(("parallel", "parallel")),
        name="ssd_conv",
    )(xbc, conv_w, conv_b.reshape(1, c))


def _exact_cumsum_dot(tri, a):
    a1 = a.astype(BF16)
    r1 = a - a1.astype(F32)
    a2 = r1.astype(BF16)
    a3 = (r1 - a2.astype(F32)).astype(BF16)
    dot = lambda t: jnp.dot(tri, t, preferred_element_type=F32)
    return dot(a1) + dot(a2) + dot(a3)


def _scan_step(xbc_ref, dt_ref, bias_ref, a_ref, state_ref, *, reverse):
    q = SCAN_Q
    lane0 = HEADS if reverse else 0
    sp = _softplus(dt_ref[0] + bias_ref[...])
    a = sp * a_ref[...]
    rowi = lax.broadcasted_iota(jnp.int32, (q, q), 0)
    coli = lax.broadcasted_iota(jnp.int32, (q, q), 1)
    causal = (rowi <= coli) if reverse else (rowi >= coli)
    first_head = coli < HEADDIM
    acum = _exact_cumsum_dot(jnp.where(causal, 1.0, 0.0).astype(BF16), a)
    acum_t = acum.T
    sp_t = sp.T
    last = 0 if reverse else q - 1
    ys = []
    for g in range(GROUPS):
        b_g = xbc_ref[0, :, D_INNER + g * STATE:D_INNER + (g + 1) * STATE]
        c_g = xbc_ref[0, :, D_INNER + GN + g * STATE:D_INNER + GN + (g + 1) * STATE]
        scores = lax.dot_general(c_g, b_g, (((1,), (1,)), ((), ())), preferred_element_type=F32)
        b_gt = b_g.astype(F32).T.astype(BF16)
        y_pairs = []
        for pr in range(HPG // 2):
            lanes = slice((g * HPG + 2 * pr) * HEADDIM, (g * HPG + 2 * pr + 2) * HEADDIM)
            st_lanes = slice(2 * pr * HEADDIM, (2 * pr + 2) * HEADDIM)
            xs_pair = xbc_ref[0, :, lanes]
            y_state = jnp.dot(c_g, state_ref[g, :, st_lanes].astype(BF16), preferred_element_type=F32)
            cols, spcols, y_diag = [], [], None
            for j in range(2):
                hl = lane0 + g * HPG + 2 * pr + j
                col_a = jnp.broadcast_to(acum[:, hl:hl + 1], (q, q))
                row_a = jnp.broadcast_to(acum_t[hl:hl + 1, :], (q, q))
                decay = jnp.where(causal, jnp.exp(col_a - row_a), 0.0)
                m_h = (scores * decay * sp_t[hl:hl + 1, :]).astype(BF16)
                keep = first_head if j == 0 else jnp.logical_not(first_head)
                xs_h = jnp.where(keep, xs_pair, jnp.zeros_like(xs_pair))
                part = jnp.dot(m_h, xs_h, preferred_element_type=F32)
                y_diag = part if y_diag is None else y_diag + part
                cols.append(col_a)
                spcols.append(jnp.broadcast_to(sp[:, hl:hl + 1], (q, q)))
            col_p = jnp.where(first_head, cols[0], cols[1])
            sp_p = jnp.where(first_head, spcols[0], spcols[1])
            e_col = jnp.exp(col_p)
            y_pairs.append(y_diag + e_col * y_state)
            w_col = jnp.exp(col_p[last:last + 1, :] - col_p)
            xw = (xs_pair.astype(F32) * (w_col * sp_p)).astype(BF16)
            cd = jnp.broadcast_to(e_col[last:last + 1, :], (STATE, q))
            state_ref[g, :, st_lanes] = (state_ref[g, :, st_lanes] * cd
                                         + jnp.dot(b_gt, xw, preferred_element_type=F32))
        ys.append(jnp.concatenate(y_pairs, axis=1))
    return ys


def _scan_fwd_kernel(xbc_ref, dt_ref, bias_ref, a_ref, h0_ref, y_ref, hfin_ref, state_ref):
    c = pl.program_id(1)

    @pl.when(c == 0)
    def _():
        state_ref[...] = h0_ref[0]

    ys = _scan_step(xbc_ref, dt_ref, bias_ref, a_ref, state_ref, reverse=False)
    for g in range(GROUPS):
        y_ref[0, :, g * 256:(g + 1) * 256] = ys[g]

    @pl.when(c == pl.num_programs(1) - 1)
    def _():
        hfin_ref[0] = state_ref[...]


def _scan_bwd_kernel(xbc_ref, dt_ref, bias_ref, a_ref, h0_ref, z_ref, yf_ref, res_ref, gate_ref,
                     dskip_ref, nw_ref, wo_ref, hnew_ref, hfin_ref, state_ref, y_buf):
    c = pl.program_id(1)

    @pl.when(c == 0)
    def _():
        state_ref[...] = h0_ref[0]

    ys = _scan_step(xbc_ref, dt_ref, bias_ref, a_ref, state_ref, reverse=True)
    for g in range(GROUPS):
        sl = slice(g * 256, (g + 1) * 256)
        xs_g = xbc_ref[0, :, sl].astype(F32)
        y_buf[:, sl] = yf_ref[0, :, sl] + ys[g] + dskip_ref[:, sl] * xs_g
    yz = y_buf[...] * _silu(z_ref[0].astype(F32))
    yn = yz * lax.rsqrt(jnp.mean(yz * yz, axis=-1, keepdims=True) + EPS) * nw_ref[...]
    out = jnp.dot(yn.astype(BF16), wo_ref[...], preferred_element_type=F32)
    hnew_ref[0] = res_ref[0] + gate_ref[0] * out

    @pl.when(c == pl.num_programs(1) - 1)
    def _():
        hfin_ref[0] = state_ref[...]


def _scan_fwd_call(xbc, dt, bias, a, h0):
    n, l, _ = xbc.shape
    nc = l // SCAN_Q
    row = lambda b, c: (b, c, 0)
    st = lambda b, c: (b, 0, 0, 0)
    st_shape = (1, GROUPS, STATE, HPG * HEADDIM)
    return pl.pallas_call(
        _scan_fwd_kernel,
        grid=(n, nc),
        in_specs=[
            pl.BlockSpec((1, SCAN_Q, CONV_DIM), row),
            pl.BlockSpec((1, SCAN_Q, DT_LANES), row),
            _const_spec((1, DT_LANES)),
            _const_spec((1, DT_LANES)),
            pl.BlockSpec(st_shape, st),
        ],
        out_specs=[
            pl.BlockSpec((1, SCAN_Q, D_INNER), row),
            pl.BlockSpec(st_shape, st),
        ],
        out_shape=[
            jax.ShapeDtypeStruct((n, l, D_INNER), F32),
            jax.ShapeDtypeStruct((n,) + st_shape[1:], F32),
        ],
        scratch_shapes=[pltpu.VMEM(st_shape[1:], F32)],
        compiler_params=_params(("parallel", "arbitrary")),
        name="ssd_scan_fwd",
    )(xbc, dt, bias, a, h0)


def _scan_bwd_call(xbc, dt, bias, a, h0, z, yf, res, gate, dskip, nw, wo):
    n, l, _ = xbc.shape
    nc = l // SCAN_Q
    row = lambda b, c: (b, nc - 1 - c, 0)
    st = lambda b, c: (b, 0, 0, 0)
    mod = lambda b, c: (b, 0, 0)
    st_shape = (1, GROUPS, STATE, HPG * HEADDIM)
    return pl.pallas_call(
        _scan_bwd_kernel,
        grid=(n, nc),
        in_specs=[
            pl.BlockSpec((1, SCAN_Q, CONV_DIM), row),
            pl.BlockSpec((1, SCAN_Q, DT_LANES), row),
            _const_spec((1, DT_LANES)),
            _const_spec((1, DT_LANES)),
            pl.BlockSpec(st_shape, st),
            pl.BlockSpec((1, SCAN_Q, D_INNER), row),
            pl.BlockSpec((1, SCAN_Q, D_INNER), row),
            pl.BlockSpec((1, SCAN_Q, D_MODEL), row),
            pl.BlockSpec((1, 1, D_MODEL), mod),
            _const_spec((1, D_INNER)),
            _const_spec((1, D_INNER)),
            _const_spec(wo.shape),
        ],
        out_specs=[
            pl.BlockSpec((1, SCAN_Q, D_MODEL), row),
            pl.BlockSpec(st_shape, st),
        ],
        out_shape=[
            jax.ShapeDtypeStruct((n, l, D_MODEL), F32),
            jax.ShapeDtypeStruct((n,) + st_shape[1:], F32),
        ],
        scratch_shapes=[pltpu.VMEM(st_shape[1:], F32), pltpu.VMEM((SCAN_Q, D_INNER), F32)],
        compiler_params=_params(("parallel", "arbitrary")),
        name="ssd_scan_bwd",
    )(xbc, dt, bias, a, h0, z, yf, res, gate, dskip, nw, wo)


def _ffn_kernel(x_ref, sh_ref, sc_ref, gt_ref, g_ref, wa_ref, wb_ref, wo_ref, gf_ref, o_ref, acc_ref,
                *, final_norm):
    x = x_ref[0]
    xm = _norm_modulate(x, g_ref[...], sh_ref[0], sc_ref[0]).astype(BF16)
    for j in range(FFN_HIDDEN // FFN_CHUNK):
        sl = slice(j * FFN_CHUNK, (j + 1) * FFN_CHUNK)
        ua = jnp.dot(xm, wa_ref[:, sl], preferred_element_type=F32)
        ub = jnp.dot(xm, wb_ref[:, sl], preferred_element_type=F32)
        act = (_silu(ua) * ub).astype(BF16)
        part = jnp.dot(act, wo_ref[sl, :], preferred_element_type=F32)
        if j == 0:
            acc_ref[...] = part
        else:
            acc_ref[...] += part
    y = x + gt_ref[0] * acc_ref[...]
    if final_norm:
        y = y * lax.rsqrt(jnp.mean(y * y, axis=-1, keepdims=True) + EPS) * gf_ref[...]
    o_ref[0] = y


def _ffn_call(h, shift, scale, gate, g, wa, wb, wo, gf, tm, final_norm):
    n, l, d = h.shape
    row = lambda b, t: (b, t, 0)
    mod = lambda b, t: (b, 0, 0)
    return pl.pallas_call(
        functools.partial(_ffn_kernel, final_norm=final_norm),
        grid=(n, l // tm),
        in_specs=[
            pl.BlockSpec((1, tm, d), row),
            pl.BlockSpec((1, 1, d), mod),
            pl.BlockSpec((1, 1, d), mod),
            pl.BlockSpec((1, 1, d), mod),
            _const_spec((1, d)),
            _const_spec(wa.shape),
            _const_spec(wb.shape),
            _const_spec(wo.shape),
            _const_spec((1, d)),
        ],
        out_specs=pl.BlockSpec((1, tm, d), row),
        out_shape=jax.ShapeDtypeStruct((n, l, d), F32),
        scratch_shapes=[pltpu.VMEM((tm, d), F32)],
        compiler_params=_params(("parallel", "parallel")),
        name="swiglu_ffn",
    )(h, shift, scale, gate, g, wa, wb, wo, gf)


def _conf_pw1_kernel(x_ref, sh_ref, sc_ref, g_ref, w_ref, b_ref, o_ref):
    xm = _norm_modulate(x_ref[0], g_ref[...], sh_ref[0], sc_ref[0]).astype(BF16)
    d = o_ref.shape[-1]
    ua = jnp.dot(xm, w_ref[:, :d], preferred_element_type=F32) + b_ref[:, :d]
    ub = jnp.dot(xm, w_ref[:, d:], preferred_element_type=F32) + b_ref[:, d:]
    o_ref[0] = ua * jax.nn.sigmoid(ub)


def _conf_pw1_call(h, shift, scale, g, w, b, tm):
    n, l, d = h.shape
    row = lambda bb, t: (bb, t, 0)
    mod = lambda bb, t: (bb, 0, 0)
    return pl.pallas_call(
        _conf_pw1_kernel,
        grid=(n, l // tm),
        in_specs=[
            pl.BlockSpec((1, tm, d), row),
            pl.BlockSpec((1, 1, d), mod),
            pl.BlockSpec((1, 1, d), mod),
            _const_spec((1, d)),
            _const_spec(w.shape),
            _const_spec(b.shape),
        ],
        out_specs=pl.BlockSpec((1, tm, d), row),
        out_shape=jax.ShapeDtypeStruct((n, l, d), F32),
        compiler_params=_params(("parallel", "parallel")),
        name="conf_pw1_glu",
    )(h, shift, scale, g, w, b)


_HPAD = 16


def _conf_conv_kernel(u_ref, up_ref, un_ref, res_ref, gate_ref, dw_ref, db_ref, lg_ref, lb_ref, w2_ref,
                      b2_ref, o_ref, hpad_ref, vbuf_ref, conv_ref):
    r = pl.program_id(1)
    nr = pl.num_programs(1)
    rr, half = CONF_ROWS, CONF_K // 2
    ch = CONF_H

    hpad_ref[:, 0:_HPAD, :] = jnp.zeros((rr, _HPAD, ch), F32)
    hpad_ref[:, _HPAD + GRID_W:, :] = jnp.zeros((rr, _HPAD, ch), F32)
    hpad_ref[:, _HPAD:_HPAD + GRID_W, :] = u_ref[0, :, :, :ch]
    vbuf_ref[half:half + rr] = u_ref[0, :, :, ch:]

    @pl.when(r > 0)
    def _():
        vbuf_ref[0:half] = up_ref[0, rr - half:, :, :]

    @pl.when(r == 0)
    def _():
        vbuf_ref[0:half] = jnp.zeros((half, GRID_W, ch), F32)

    @pl.when(r < nr - 1)
    def _():
        vbuf_ref[half + rr:] = un_ref[0, :half, :, :]

    @pl.when(r == nr - 1)
    def _():
        vbuf_ref[half + rr:] = jnp.zeros((half, GRID_W, ch), F32)

    def row_body(i, carry):
        acc_h = jnp.broadcast_to(db_ref[:, :ch], (GRID_W, ch))
        acc_v = jnp.broadcast_to(db_ref[:, ch:], (GRID_W, ch))
        for k in range(CONF_K):
            s = _HPAD - half + k
            acc_h = acc_h + hpad_ref[i, s:s + GRID_W, :] * dw_ref[k:k + 1, :ch]
            acc_v = acc_v + vbuf_ref[i + k] * dw_ref[k:k + 1, ch:]
        conv_ref[i, :, :ch] = acc_h
        conv_ref[i, :, ch:] = acc_v
        return carry

    lax.fori_loop(0, rr, row_body, 0)

    def ln_body(i, carry):
        v = conv_ref[i]
        mu = jnp.mean(v, axis=-1, keepdims=True)
        vc = v - mu
        var = jnp.mean(vc * vc, axis=-1, keepdims=True)
        y = _silu(vc * lax.rsqrt(var + EPS) * lg_ref[...] + lb_ref[...]).astype(BF16)
        out = jnp.dot(y, w2_ref[...], preferred_element_type=F32) + b2_ref[...]
        o_ref[0, i] = res_ref[0, i] + gate_ref[0] * out
        return carry

    lax.fori_loop(0, rr, ln_body, 0)


def _conf_conv_call(u, res, gate, dw, db, lg, lb, w2, b2):
    n, l, c = u.shape
    rows = l // GRID_W
    nrb = rows // CONF_ROWS
    u4 = u.reshape(n, rows, GRID_W, c)
    res4 = res.reshape(n, rows, GRID_W, c)
    blk = (1, CONF_ROWS, GRID_W, c)
    hblk = (1, CONF_ROWS, GRID_W, CONF_H)
    out = pl.pallas_call(
        _conf_conv_kernel,
        grid=(n, nrb),
        in_specs=[
            pl.BlockSpec(blk, lambda b, r: (b, r, 0, 0)),
            pl.BlockSpec(hblk, lambda b, r: (b, jnp.maximum(r - 1, 0), 0, 1)),
            pl.BlockSpec(hblk, lambda b, r: (b, jnp.minimum(r + 1, nrb - 1), 0, 1)),
            pl.BlockSpec(blk, lambda b, r: (b, r, 0, 0)),
            pl.BlockSpec((1, 1, c), lambda b, r: (b, 0, 0)),
            _const_spec(dw.shape),
            _const_spec((1, c)),
            _const_spec((1, c)),
            _const_spec((1, c)),
            _const_spec(w2.shape),
            _const_spec((1, c)),
        ],
        out_specs=pl.BlockSpec(blk, lambda b, r: (b, r, 0, 0)),
        out_shape=jax.ShapeDtypeStruct((n, rows, GRID_W, c), F32),
        scratch_shapes=[
            pltpu.VMEM((CONF_ROWS, GRID_W + 2 * _HPAD, CONF_H), F32),
            pltpu.VMEM((CONF_ROWS + 2 * (CONF_K // 2), GRID_W, CONF_H), F32),
            pltpu.VMEM((CONF_ROWS, GRID_W, c), F32),
        ],
        compiler_params=_params(("parallel", "parallel")),
        name="conf_conv_ln_pw2",
    )(u4, u4, u4, res4, gate, dw, db.reshape(1, c), lg.reshape(1, c), lb.reshape(1, c), w2,
      b2.reshape(1, c))
    return out.reshape(n, l, c)


def _ssd_mixer(h, shift, scale, gate, g, h0_f, h0_b, p, tm):
    wz, wx, wd, conv_w, conv_b, bias, a, dskip, nw, wo = p
    z, xbc_raw, dt = _inproj_call(h, shift, scale, g, wz, wx, wd, tm)
    xbc = _ssd_conv_call(xbc_raw, conv_w, conv_b)
    yf, hf = _scan_fwd_call(xbc, dt, bias, a, h0_f)
    hnew, hb = _scan_bwd_call(xbc, dt, bias, a, h0_b, z, yf, h, gate, dskip, nw, wo)
    return hnew, hf, hb


def kernel(x, c, ctx, c_ctx, ada_w, ada_b, norm_mix_g, norm_ffn_g, final_norm_g, ssd_w_in, ssd_conv_w,
           ssd_conv_b, ssd_dt_bias_f, ssd_dt_bias_b, ssd_a_log_f, ssd_a_log_b, ssd_d_skip, ssd_norm_w,
           ssd_w_out, conf_w_pw1, conf_b_pw1, conf_dw_w, conf_dw_b, conf_ln_g, conf_ln_b, conf_w_pw2,
           conf_b_pw2, ffn_w_in, ffn_w_out):
    n, l, d = x.shape
    depth = ada_w.shape[0]
    lctx = ctx.shape[1]

    mod_rows = 16
    cc = jnp.concatenate([c, c_ctx[None, :], jnp.zeros((mod_rows - n - 1, d), F32)], axis=0)
    mod = _ada_call(cc, ada_w, ada_b)

    def mods(i):
        lat = [mod[i, :n, k * d:(k + 1) * d][:, None, :] for k in range(6)]
        cx = [jnp.broadcast_to(mod[i, n:n + 1, k * d:(k + 1) * d][:, None, :], (n, 1, d)) for k in range(6)]
        return lat, cx

    def ffn(h, sh, sc, gt, i, tm, final_norm):
        wa = ffn_w_in[i, :, :FFN_HIDDEN].astype(BF16)
        wb = ffn_w_in[i, :, FFN_HIDDEN:].astype(BF16)
        wo = ffn_w_out[i].astype(BF16)
        return _ffn_call(h, sh, sc, gt, norm_ffn_g[i][None, :], wa, wb, wo, final_norm_g[None, :], tm,
                         final_norm)

    h_lat, h_ctx = x, ctx

    lat, cx = mods(0)
    w_in = ssd_w_in[0]
    pad = jnp.zeros((DT_LANES - 2 * HEADS,), F32)
    expand = lambda v: jnp.repeat(v, HEADDIM)[None, :]
    p = (
        w_in[:, :D_INNER].astype(BF16),
        w_in[:, D_INNER:D_INNER + CONV_DIM].astype(BF16),
        jnp.pad(w_in[:, D_INNER + CONV_DIM:], ((0, 0), (0, DT_LANES - 2 * HEADS))).astype(BF16),
        ssd_conv_w[0], ssd_conv_b[0],
        jnp.concatenate([ssd_dt_bias_f[0], ssd_dt_bias_b[0], pad])[None, :],
        jnp.concatenate([-jnp.exp(ssd_a_log_f[0]), -jnp.exp(ssd_a_log_b[0]), pad])[None, :],
        expand(ssd_d_skip[0]), ssd_norm_w[0][None, :], ssd_w_out[0].astype(BF16),
    )
    g_mix = norm_mix_g[0][None, :]
    h_zero = jnp.zeros((n, GROUPS, STATE, HPG * HEADDIM), F32)
    h_ctx, hf_ctx, hb_ctx = _ssd_mixer(h_ctx, cx[0], cx[1], cx[2], g_mix, h_zero, h_zero, p, lctx)
    h_lat, _, _ = _ssd_mixer(h_lat, lat[0], lat[1], lat[2], g_mix, hf_ctx, hb_ctx, p, 512)
    h_lat = ffn(h_lat, lat[3], lat[4], lat[5], 0, 512, False)
    h_ctx = ffn(h_ctx, cx[3], cx[4], cx[5], 0, lctx, False)

    lat, cx = mods(1)
    u = _conf_pw1_call(h_lat, lat[0], lat[1], norm_mix_g[1][None, :], conf_w_pw1[0].astype(BF16),
                       conf_b_pw1[0][None, :], 512)
    h_lat = _conf_conv_call(u, h_lat, lat[2], conf_dw_w[0], conf_dw_b[0], conf_ln_g[0], conf_ln_b[0],
                            conf_w_pw2[0].astype(BF16), conf_b_pw2[0])
    return ffn(h_lat, lat[3], lat[4], lat[5], 1, 512, True)
```

```python
import functools
import math

import jax
import jax.numpy as jnp
from jax import lax
from jax.experimental import pallas as pl
from jax.experimental.pallas import tpu as pltpu

F32 = jnp.float32
BF16 = jnp.bfloat16

EPS = 1e-6
LOG2E = math.log2(math.e)
D_MODEL = 1024
GRID_W = 64
D_INNER = 2048
HEADDIM = 64
HEADS = 32
GROUPS = 8
HPG = 4
STATE = 128
GN = GROUPS * STATE
CONV_K = 5
CONV_DIM = D_INNER + 2 * GN
DT_LANES = 128
SCAN_Q = 128
HALO = 16
CONF_K = 31
CONF_H = 512
CONF_ROWS = 16
SUBLANES = 8
FFN_HIDDEN = 2816
FFN_CHUNK = 256

VMEM_LIMIT = 56 * 1024 * 1024


def _const_spec(shape):
    nd = len(shape)
    return pl.BlockSpec(shape, lambda *_: (0,) * nd, pipeline_mode=pl.Buffered(1))


def _params(sem):
    return pltpu.CompilerParams(dimension_semantics=sem, vmem_limit_bytes=VMEM_LIMIT)


def _silu(x):
    return x * jax.nn.sigmoid(x)


def _softplus(x):
    return jnp.maximum(x, 0.0) + jnp.log1p(jnp.exp(-jnp.abs(x)))


def _norm_modulate(x, g, shift, scale):
    y = x * lax.rsqrt(jnp.mean(x * x, axis=-1, keepdims=True) + EPS) * g
    return y * (1.0 + scale) + shift


def _ada_kernel(c_ref, w_ref, b_ref, o_ref):
    s = _silu(c_ref[...]).astype(BF16)
    o_ref[...] = jnp.dot(s, w_ref[...].astype(BF16), preferred_element_type=F32) + b_ref[...]


def _ada_call(cc, ada_w, ada_b):
    depth, d, n6 = ada_w.shape
    tn = 1536
    rows = cc.shape[0]
    return pl.pallas_call(
        _ada_kernel,
        grid=(depth, n6 // tn),
        in_specs=[
            pl.BlockSpec((rows, d), lambda i, j: (0, 0)),
            pl.BlockSpec((None, d, tn), lambda i, j: (i, 0, j)),
            pl.BlockSpec((None, 1, tn), lambda i, j: (i, 0, j)),
        ],
        out_specs=pl.BlockSpec((None, rows, tn), lambda i, j: (i, 0, j)),
        out_shape=jax.ShapeDtypeStruct((depth, rows, n6), F32),
        compiler_params=_params(("arbitrary", "arbitrary")),
        name="ada_params",
    )(cc, ada_w, ada_b.reshape(depth, 1, n6))


def _inproj_kernel(x_ref, xp_ref, xn_ref, sh_ref, sc_ref, g_ref, wz_ref, wx_ref, wd_ref, cw_ref, cb_ref,
                   z_ref, xbc_ref, dt_ref, xs_ref, xmp_ref, raw_ref, nat_ref):
    t = pl.program_id(1)
    nt = pl.num_programs(1)
    tm, d = x_ref.shape[1], x_ref.shape[2]
    rows = tm + 2 * HALO
    p = rows // SUBLANES
    lane = 128
    nm = lambda v: _norm_modulate(v, g_ref[...], sh_ref[0], sc_ref[0])

    def stage(lo, v):
        for c in range(d // lane):
            xs_ref[c, lo:lo + v.shape[0], :] = v[:, c * lane:(c + 1) * lane]

    main = nm(x_ref[0])
    stage(HALO, main)

    @pl.when(t > 0)
    def _():
        stage(0, nm(xp_ref[0, 0]))

    @pl.when(t == 0)
    def _():
        stage(0, jnp.zeros((HALO, d), F32))

    @pl.when(t < nt - 1)
    def _():
        stage(HALO + tm, nm(xn_ref[0, 0]))

    @pl.when(t == nt - 1)
    def _():
        stage(HALO + tm, jnp.zeros((HALO, d), F32))

    for c in range(d // lane):
        tiles = [xs_ref[c, pl.ds(i, SUBLANES, stride=p), :] for i in range(p)]
        xmp_ref[:, c * lane:(c + 1) * lane] = jnp.concatenate(tiles, axis=0).astype(BF16)

    nb = 512
    half = CONV_K // 2
    nblk = 4 if p % 4 == 0 else 2
    rb = rows // nblk
    xm = main.astype(BF16)
    dt_ref[0] = jnp.dot(xm, wd_ref[...], preferred_element_type=F32)
    for j in range(CONV_DIM // nb):
        if j % 2 == 0:
            zc = slice(j // 2 * nb, (j // 2 + 1) * nb)
            z_ref[0, :, zc] = jnp.dot(xm, wz_ref[:, zc], preferred_element_type=F32).astype(BF16)
        cols = slice(j * nb, (j + 1) * nb)
        slot = j % 2
        raw = jnp.dot(xmp_ref[...], wx_ref[:, cols], preferred_element_type=F32)
        lead = half * SUBLANES
        for m in range(half):
            raw_ref[slot, m * SUBLANES:(m + 1) * SUBLANES, :] = pltpu.roll(
                raw[rows - lead + m * SUBLANES:rows - lead + (m + 1) * SUBLANES], 1, axis=0)
            raw_ref[slot, lead + rows + m * SUBLANES:lead + rows + (m + 1) * SUBLANES, :] = pltpu.roll(
                raw[m * SUBLANES:(m + 1) * SUBLANES], SUBLANES - 1, axis=0)
        raw_ref[slot, lead:lead + rows, :] = raw
        for r in range(nblk):
            for c in range(nb // lane):
                lc = slice(c * lane, (c + 1) * lane)
                wc = slice(j * nb + c * lane, j * nb + (c + 1) * lane)
                acc = cb_ref[:, wc]
                for k in range(CONV_K):
                    lo = k * SUBLANES + r * rb
                    acc = acc + raw_ref[slot, lo:lo + rb, lc] * cw_ref[k:k + 1, wc]
                act = _silu(acc)
                for i in range(rb // SUBLANES):
                    nat_ref[slot, c, pl.ds(r * rb // SUBLANES + i, SUBLANES, stride=p), :] = (
                        act[i * SUBLANES:(i + 1) * SUBLANES])
        xbc_ref[0, :, cols] = jnp.concatenate(
            [nat_ref[slot, c, HALO:HALO + tm, :] for c in range(nb // lane)], axis=1).astype(BF16)


def _inproj_call(h, shift, scale, g, wz, wx, wd, conv_w, conv_b, tm):
    n, l, d = h.shape
    hb = tm // HALO
    nhb = l // HALO
    h16 = h.reshape(n, nhb, HALO, d)
    row = lambda b, t: (b, t, 0)
    mod = lambda b, t: (b, 0, 0)
    return pl.pallas_call(
        _inproj_kernel,
        grid=(n, l // tm),
        in_specs=[
            pl.BlockSpec((1, tm, d), row),
            pl.BlockSpec((1, 1, HALO, d), lambda b, t: (b, jnp.maximum(t * hb - 1, 0), 0, 0)),
            pl.BlockSpec((1, 1, HALO, d), lambda b, t: (b, jnp.minimum((t + 1) * hb, nhb - 1), 0, 0)),
            pl.BlockSpec((1, 1, d), mod),
            pl.BlockSpec((1, 1, d), mod),
            _const_spec((1, d)),
            _const_spec(wz.shape),
            _const_spec(wx.shape),
            _const_spec(wd.shape),
            _const_spec(conv_w.shape),
            _const_spec((1, CONV_DIM)),
        ],
        out_specs=[
            pl.BlockSpec((1, tm, D_INNER), row),
            pl.BlockSpec((1, tm, CONV_DIM), row),
            pl.BlockSpec((1, tm, DT_LANES), row),
        ],
        out_shape=[
            jax.ShapeDtypeStruct((n, l, D_INNER), BF16),
            jax.ShapeDtypeStruct((n, l, CONV_DIM), BF16),
            jax.ShapeDtypeStruct((n, l, DT_LANES), F32),
        ],
        scratch_shapes=[
            pltpu.VMEM((d // 128, tm + 2 * HALO, 128), F32),
            pltpu.VMEM((tm + 2 * HALO, d), BF16),
            pltpu.VMEM((2, tm + 2 * HALO + 2 * (CONV_K // 2) * SUBLANES, 512), F32),
            pltpu.VMEM((2, 4, tm + 2 * HALO, 128), F32),
        ],
        compiler_params=_params(("parallel", "arbitrary")),
        name="ssd_inproj_conv",
    )(h, h16, h16, shift, scale, g, wz, wx, wd, conv_w, conv_b.reshape(1, CONV_DIM))


def _exact_cumsum_dot(tri, a):
    a1 = a.astype(BF16)
    r1 = a - a1.astype(F32)
    a2 = r1.astype(BF16)
    a3 = (r1 - a2.astype(F32)).astype(BF16)
    dot = lambda t: jnp.dot(tri, t, preferred_element_type=F32)
    return dot(a1) + dot(a2) + dot(a3)


def _decay_kernel(dt_ref, bias_ref, a_ref, ac_ref, rowt_ref, wt_ref):
    q = SCAN_Q
    sp = _softplus(dt_ref[0] + bias_ref[...])
    a = sp * a_ref[...]
    rowi = lax.broadcasted_iota(jnp.int32, (q, q), 0)
    coli = lax.broadcasted_iota(jnp.int32, (q, q), 1)
    fwd = coli < HEADS
    tri_f = jnp.where(rowi >= coli, 1.0, 0.0).astype(BF16)
    tri_b = jnp.where(rowi <= coli, 1.0, 0.0).astype(BF16)
    ac = jnp.where(fwd, _exact_cumsum_dot(tri_f, a), _exact_cumsum_dot(tri_b, a)) * LOG2E
    d = jnp.log2(sp) - ac
    total = jnp.where(fwd, ac[q - 1:q, :], ac[0:1, :])
    ac_ref[0] = ac
    rowt_ref[0, 0] = (-d).T
    wt_ref[0, 0] = jnp.exp2(total + d).T


def _decay_call(dt, bias, a):
    n, l, _ = dt.shape
    nc = l // SCAN_Q
    row = lambda b, c: (b, c, 0)
    tab = lambda b, c: (b, c, 0, 0)
    return pl.pallas_call(
        _decay_kernel,
        grid=(n, nc),
        in_specs=[
            pl.BlockSpec((1, SCAN_Q, DT_LANES), row),
            _const_spec((1, DT_LANES)),
            _const_spec((1, DT_LANES)),
        ],
        out_specs=[
            pl.BlockSpec((1, SCAN_Q, DT_LANES), row),
            pl.BlockSpec((1, 1, DT_LANES, SCAN_Q), tab),
            pl.BlockSpec((1, 1, DT_LANES, SCAN_Q), tab),
        ],
        out_shape=[
            jax.ShapeDtypeStruct((n, l, DT_LANES), F32),
            jax.ShapeDtypeStruct((n, nc, DT_LANES, SCAN_Q), F32),
            jax.ShapeDtypeStruct((n, nc, DT_LANES, SCAN_Q), F32),
        ],
        compiler_params=_params(("parallel", "parallel")),
        name="ssd_decay_tables",
    )(dt, bias, a)


def _scan_step(xbc_ref, ac_ref, rowt_ref, wt_ref, state_ref, sc_ref, bt_ref, yst_ref,
               emit, *, reverse):
    q = SCAN_Q
    lane0 = HEADS if reverse else 0
    rowi = lax.broadcasted_iota(jnp.int32, (q, q), 0)
    coli = lax.broadcasted_iota(jnp.int32, (q, q), 1)
    causal = (rowi <= coli) if reverse else (rowi >= coli)
    first_head = coli < HEADDIM
    last = 0 if reverse else q - 1
    for g in range(GROUPS):
        b_g = xbc_ref[0, :, D_INNER + g * STATE:D_INNER + (g + 1) * STATE]
        c_g = xbc_ref[0, :, D_INNER + GN + g * STATE:D_INNER + GN + (g + 1) * STATE]
        sc_ref[g] = lax.dot_general(c_g, b_g, (((1,), (1,)), ((), ())), preferred_element_type=F32)
        bt_ref[g] = b_g.astype(F32).T
        yst_ref[:, g * HPG * HEADDIM:(g + 1) * HPG * HEADDIM] = jnp.dot(
            c_g, state_ref[g].astype(BF16), preferred_element_type=F32)
    for g in range(GROUPS):
        scores = sc_ref[g]
        b_gt = bt_ref[g]
        for pr in range(HPG // 2):
            lanes = slice((g * HPG + 2 * pr) * HEADDIM, (g * HPG + 2 * pr + 2) * HEADDIM)
            st_lanes = slice(2 * pr * HEADDIM, (2 * pr + 2) * HEADDIM)
            xs_pair = xbc_ref[0, :, lanes]
            state = state_ref[g, :, st_lanes]
            y_state = yst_ref[:, lanes]
            cols, y_diag, upd = [], None, None
            for j in range(2):
                hl = lane0 + g * HPG + 2 * pr + j
                col_a = jnp.broadcast_to(ac_ref[0, :, hl:hl + 1], (q, q))
                seg = col_a - rowt_ref[0, 0, hl:hl + 1, :]
                m_h = (scores * jnp.where(causal, jnp.exp2(seg), 0.0)).astype(BF16)
                keep = first_head if j == 0 else jnp.logical_not(first_head)
                xs_h = jnp.where(keep, xs_pair, jnp.zeros_like(xs_pair))
                part = jnp.dot(m_h, xs_h, preferred_element_type=F32)
                y_diag = part if y_diag is None else y_diag + part
                bw_h = (b_gt * wt_ref[0, 0, hl:hl + 1, :]).astype(BF16)
                part = jnp.dot(bw_h, xs_h, preferred_element_type=F32)
                upd = part if upd is None else upd + part
                cols.append(col_a)
            e_col = jnp.exp2(jnp.where(first_head, cols[0], cols[1]))
            emit(lanes, y_diag + e_col * y_state, xs_pair)
            cd = jnp.broadcast_to(e_col[last:last + 1, :], (STATE, q))
            state_ref[g, :, st_lanes] = state * cd + upd


def _scan_fwd_kernel(xbc_ref, ac_ref, rowt_ref, wt_ref, h0_ref, y_ref, hfin_ref,
                     state_ref, sc_ref, bt_ref, yst_ref):
    c = pl.program_id(1)

    @pl.when(c == 0)
    def _():
        state_ref[...] = h0_ref[0]

    def emit(lanes, y, xs):
        y_ref[0, :, lanes] = y

    _scan_step(xbc_ref, ac_ref, rowt_ref, wt_ref, state_ref, sc_ref, bt_ref, yst_ref,
               emit, reverse=False)

    @pl.when(c == pl.num_programs(1) - 1)
    def _():
        hfin_ref[0] = state_ref[...]


def _scan_bwd_kernel(xbc_ref, ac_ref, rowt_ref, wt_ref, h0_ref, z_ref, yf_ref, res_ref, gate_ref,
                     dskip_ref, nw_ref, wo_ref, hnew_ref, hfin_ref, state_ref, sc_ref, bt_ref, yst_ref,
                     y_buf):
    c = pl.program_id(1)

    @pl.when(c == 0)
    def _():
        state_ref[...] = h0_ref[0]

    def emit(lanes, y, xs):
        y_buf[:, lanes] = yf_ref[0, :, lanes] + y + dskip_ref[:, lanes] * xs.astype(F32)

    _scan_step(xbc_ref, ac_ref, rowt_ref, wt_ref, state_ref, sc_ref, bt_ref, yst_ref,
               emit, reverse=True)
    yz = y_buf[...] * _silu(z_ref[0].astype(F32))
    yn = yz * lax.rsqrt(jnp.mean(yz * yz, axis=-1, keepdims=True) + EPS) * nw_ref[...]
    out = jnp.dot(yn.astype(BF16), wo_ref[...], preferred_element_type=F32)
    hnew_ref[0] = res_ref[0] + gate_ref[0] * out

    @pl.when(c == pl.num_programs(1) - 1)
    def _():
        hfin_ref[0] = state_ref[...]


_STATE_BLOCK = (1, GROUPS, STATE, HPG * HEADDIM)


def _scan_scratch():
    return [
        pltpu.VMEM(_STATE_BLOCK[1:], F32),
        pltpu.VMEM((GROUPS, SCAN_Q, SCAN_Q), F32),
        pltpu.VMEM((GROUPS, STATE, SCAN_Q), F32),
        pltpu.VMEM((SCAN_Q, D_INNER), F32),
    ]


def _scan_fwd_call(xbc, ac, rowt, wt, h0):
    n, l, _ = xbc.shape
    nc = l // SCAN_Q
    row = lambda b, c: (b, c, 0)
    tab = lambda b, c: (b, c, 0, 0)
    st = lambda b, c: (b, 0, 0, 0)
    return pl.pallas_call(
        _scan_fwd_kernel,
        grid=(n, nc),
        in_specs=[
            pl.BlockSpec((1, SCAN_Q, CONV_DIM), row),
            pl.BlockSpec((1, SCAN_Q, DT_LANES), row),
            pl.BlockSpec((1, 1, DT_LANES, SCAN_Q), tab),
            pl.BlockSpec((1, 1, DT_LANES, SCAN_Q), tab),
            pl.BlockSpec(_STATE_BLOCK, st),
        ],
        out_specs=[
            pl.BlockSpec((1, SCAN_Q, D_INNER), row),
            pl.BlockSpec(_STATE_BLOCK, st),
        ],
        out_shape=[
            jax.ShapeDtypeStruct((n, l, D_INNER), F32),
            jax.ShapeDtypeStruct((n,) + _STATE_BLOCK[1:], F32),
        ],
        scratch_shapes=_scan_scratch(),
        compiler_params=_params(("parallel", "arbitrary")),
        name="ssd_scan_fwd",
    )(xbc, ac, rowt, wt, h0)


def _scan_bwd_call(xbc, ac, rowt, wt, h0, z, yf, res, gate, dskip, nw, wo):
    n, l, _ = xbc.shape
    nc = l // SCAN_Q
    row = lambda b, c: (b, nc - 1 - c, 0)
    tab = lambda b, c: (b, nc - 1 - c, 0, 0)
    st = lambda b, c: (b, 0, 0, 0)
    mod = lambda b, c: (b, 0, 0)
    return pl.pallas_call(
        _scan_bwd_kernel,
        grid=(n, nc),
        in_specs=[
            pl.BlockSpec((1, SCAN_Q, CONV_DIM), row),
            pl.BlockSpec((1, SCAN_Q, DT_LANES), row),
            pl.BlockSpec((1, 1, DT_LANES, SCAN_Q), tab),
            pl.BlockSpec((1, 1, DT_LANES, SCAN_Q), tab),
            pl.BlockSpec(_STATE_BLOCK, st),
            pl.BlockSpec((1, SCAN_Q, D_INNER), row),
            pl.BlockSpec((1, SCAN_Q, D_INNER), row),
            pl.BlockSpec((1, SCAN_Q, D_MODEL), row),
            pl.BlockSpec((1, 1, D_MODEL), mod),
            _const_spec((1, D_INNER)),
            _const_spec((1, D_INNER)),
            _const_spec(wo.shape),
        ],
        out_specs=[
            pl.BlockSpec((1, SCAN_Q, D_MODEL), row),
            pl.BlockSpec(_STATE_BLOCK, st),
        ],
        out_shape=[
            jax.ShapeDtypeStruct((n, l, D_MODEL), F32),
            jax.ShapeDtypeStruct((n,) + _STATE_BLOCK[1:], F32),
        ],
        scratch_shapes=_scan_scratch() + [pltpu.VMEM((SCAN_Q, D_INNER), F32)],
        compiler_params=_params(("parallel", "arbitrary")),
        name="ssd_scan_bwd",
    )(xbc, ac, rowt, wt, h0, z, yf, res, gate, dskip, nw, wo)


def _ffn_kernel(x_ref, sh_ref, sc_ref, gt_ref, g_ref, wa_ref, wb_ref, wo_ref, gf_ref, o_ref, acc_ref,
                *, final_norm):
    x = x_ref[0]
    xm = _norm_modulate(x, g_ref[...], sh_ref[0], sc_ref[0]).astype(BF16)
    for j in range(FFN_HIDDEN // FFN_CHUNK):
        sl = slice(j * FFN_CHUNK, (j + 1) * FFN_CHUNK)
        ua = jnp.dot(xm, wa_ref[:, sl], preferred_element_type=F32)
        ub = jnp.dot(xm, wb_ref[:, sl], preferred_element_type=F32)
        act = (_silu(ua) * ub).astype(BF16)
        part = jnp.dot(act, wo_ref[sl, :], preferred_element_type=F32)
        if j == 0:
            acc_ref[...] = part
        else:
            acc_ref[...] += part
    y = x + gt_ref[0] * acc_ref[...]
    if final_norm:
        y = y * lax.rsqrt(jnp.mean(y * y, axis=-1, keepdims=True) + EPS) * gf_ref[...]
    o_ref[0] = y


def _ffn_call(h, shift, scale, gate, g, wa, wb, wo, gf, tm, final_norm):
    n, l, d = h.shape
    row = lambda b, t: (b, t, 0)
    mod = lambda b, t: (b, 0, 0)
    return pl.pallas_call(
        functools.partial(_ffn_kernel, final_norm=final_norm),
        grid=(n, l // tm),
        in_specs=[
            pl.BlockSpec((1, tm, d), row),
            pl.BlockSpec((1, 1, d), mod),
            pl.BlockSpec((1, 1, d), mod),
            pl.BlockSpec((1, 1, d), mod),
            _const_spec((1, d)),
            _const_spec(wa.shape),
            _const_spec(wb.shape),
            _const_spec(wo.shape),
            _const_spec((1, d)),
        ],
        out_specs=pl.BlockSpec((1, tm, d), row),
        out_shape=jax.ShapeDtypeStruct((n, l, d), F32),
        scratch_shapes=[pltpu.VMEM((tm, d), F32)],
        compiler_params=_params(("parallel", "parallel")),
        name="swiglu_ffn",
    )(h, shift, scale, gate, g, wa, wb, wo, gf)


def _conf_pw1_kernel(x_ref, sh_ref, sc_ref, g_ref, w_ref, b_ref, o_ref):
    xm = _norm_modulate(x_ref[0], g_ref[...], sh_ref[0], sc_ref[0]).astype(BF16)
    d = o_ref.shape[-1]
    ua = jnp.dot(xm, w_ref[:, :d], preferred_element_type=F32) + b_ref[:, :d]
    ub = jnp.dot(xm, w_ref[:, d:], preferred_element_type=F32) + b_ref[:, d:]
    o_ref[0] = ua * jax.nn.sigmoid(ub)


def _conf_pw1_call(h, shift, scale, g, w, b, tm):
    n, l, d = h.shape
    row = lambda bb, t: (bb, t, 0)
    mod = lambda bb, t: (bb, 0, 0)
    return pl.pallas_call(
        _conf_pw1_kernel,
        grid=(n, l // tm),
        in_specs=[
            pl.BlockSpec((1, tm, d), row),
            pl.BlockSpec((1, 1, d), mod),
            pl.BlockSpec((1, 1, d), mod),
            _const_spec((1, d)),
            _const_spec(w.shape),
            _const_spec(b.shape),
        ],
        out_specs=pl.BlockSpec((1, tm, d), row),
        out_shape=jax.ShapeDtypeStruct((n, l, d), F32),
        compiler_params=_params(("parallel", "parallel")),
        name="conf_pw1_glu",
    )(h, shift, scale, g, w, b)


_HPAD = 16
_HSPAN = GRID_W + 2 * _HPAD - SUBLANES


def _conf_conv_kernel(u_ref, up_ref, un_ref, res_ref, gate_ref, dw_ref, db_ref, lg_ref, lb_ref, w2_ref,
                      b2_ref, o_ref, hpad_ref, phase_ref, vbuf_ref, conv_ref):
    r = pl.program_id(1)
    nr = pl.num_programs(1)
    rr, half = CONF_ROWS, CONF_K // 2
    ch = CONF_H

    hpad_ref[:, 0:_HPAD, :] = jnp.zeros((rr, _HPAD, ch), F32)
    hpad_ref[:, _HPAD + GRID_W:, :] = jnp.zeros((rr, _HPAD, ch), F32)
    hpad_ref[:, _HPAD:_HPAD + GRID_W, :] = u_ref[0, :, :, :ch]
    vbuf_ref[half:half + rr] = u_ref[0, :, :, ch:]

    @pl.when(r > 0)
    def _():
        vbuf_ref[0:half] = up_ref[0, rr - half:, :, :]

    @pl.when(r == 0)
    def _():
        vbuf_ref[0:half] = jnp.zeros((half, GRID_W, ch), F32)

    @pl.when(r < nr - 1)
    def _():
        vbuf_ref[half + rr:] = un_ref[0, :half, :, :]

    @pl.when(r == nr - 1)
    def _():
        vbuf_ref[half + rr:] = jnp.zeros((half, GRID_W, ch), F32)

    def row_body(i, carry):
        for ph in range(1, SUBLANES):
            phase_ref[ph] = hpad_ref[i, ph:ph + _HSPAN, :]
        acc_h = jnp.broadcast_to(db_ref[:, :ch], (GRID_W, ch))
        acc_v = jnp.broadcast_to(db_ref[:, ch:], (GRID_W, ch))
        for k in range(CONF_K):
            s = _HPAD - half + k
            ph, al = s % SUBLANES, s - s % SUBLANES
            tap = hpad_ref[i, al:al + GRID_W, :] if ph == 0 else phase_ref[ph, al:al + GRID_W, :]
            acc_h = acc_h + tap * dw_ref[k:k + 1, :ch]
            acc_v = acc_v + vbuf_ref[i + k] * dw_ref[k:k + 1, ch:]
        base = pl.multiple_of(i * GRID_W, GRID_W)
        conv_ref[pl.ds(base, GRID_W), :ch] = acc_h
        conv_ref[pl.ds(base, GRID_W), ch:] = acc_v
        return carry

    lax.fori_loop(0, rr, row_body, 0)

    tr = 256
    for c in range(rr * GRID_W // tr):
        v = conv_ref[c * tr:(c + 1) * tr, :]
        mu = jnp.mean(v, axis=-1, keepdims=True)
        vc = v - mu
        var = jnp.mean(vc * vc, axis=-1, keepdims=True)
        y = _silu(vc * lax.rsqrt(var + EPS) * lg_ref[...] + lb_ref[...]).astype(BF16)
        out = jnp.dot(y, w2_ref[...], preferred_element_type=F32) + b2_ref[...]
        o_ref[0, c * tr:(c + 1) * tr, :] = res_ref[0, c * tr:(c + 1) * tr, :] + gate_ref[0] * out


def _conf_conv_call(u, res, gate, dw, db, lg, lb, w2, b2):
    n, l, c = u.shape
    rows = l // GRID_W
    nrb = rows // CONF_ROWS
    tok = CONF_ROWS * GRID_W
    u4 = u.reshape(n, rows, GRID_W, c)
    blk = (1, CONF_ROWS, GRID_W, c)
    hblk = (1, CONF_ROWS, GRID_W, CONF_H)
    flat = pl.BlockSpec((1, tok, c), lambda b, r: (b, r, 0))
    return pl.pallas_call(
        _conf_conv_kernel,
        grid=(n, nrb),
        in_specs=[
            pl.BlockSpec(blk, lambda b, r: (b, r, 0, 0)),
            pl.BlockSpec(hblk, lambda b, r: (b, jnp.maximum(r - 1, 0), 0, 1)),
            pl.BlockSpec(hblk, lambda b, r: (b, jnp.minimum(r + 1, nrb - 1), 0, 1)),
            flat,
            pl.BlockSpec((1, 1, c), lambda b, r: (b, 0, 0)),
            _const_spec(dw.shape),
            _const_spec((1, c)),
            _const_spec((1, c)),
            _const_spec((1, c)),
            _const_spec(w2.shape),
            _const_spec((1, c)),
        ],
        out_specs=flat,
        out_shape=jax.ShapeDtypeStruct((n, l, c), F32),
        scratch_shapes=[
            pltpu.VMEM((CONF_ROWS, GRID_W + 2 * _HPAD, CONF_H), F32),
            pltpu.VMEM((SUBLANES, _HSPAN, CONF_H), F32),
            pltpu.VMEM((CONF_ROWS + 2 * (CONF_K // 2), GRID_W, CONF_H), F32),
            pltpu.VMEM((tok, c), F32),
        ],
        compiler_params=_params(("parallel", "parallel")),
        name="conf_conv_ln_pw2",
    )(u4, u4, u4, res, gate, dw, db.reshape(1, c), lg.reshape(1, c), lb.reshape(1, c), w2,
      b2.reshape(1, c))


def _ssd_mixer(h, shift, scale, gate, g, h0_f, h0_b, p, tm):
    wz, wx, wd, conv_w, conv_b, bias, a, dskip, nw, wo = p
    z, xbc, dt = _inproj_call(h, shift, scale, g, wz, wx, wd, conv_w, conv_b, tm)
    ac, rowt, wt = _decay_call(dt, bias, a)
    yf, hf = _scan_fwd_call(xbc, ac, rowt, wt, h0_f)
    hnew, hb = _scan_bwd_call(xbc, ac, rowt, wt, h0_b, z, yf, h, gate, dskip, nw, wo)
    return hnew, hf, hb


def kernel(x, c, ctx, c_ctx, ada_w, ada_b, norm_mix_g, norm_ffn_g, final_norm_g, ssd_w_in, ssd_conv_w,
           ssd_conv_b, ssd_dt_bias_f, ssd_dt_bias_b, ssd_a_log_f, ssd_a_log_b, ssd_d_skip, ssd_norm_w,
           ssd_w_out, conf_w_pw1, conf_b_pw1, conf_dw_w, conf_dw_b, conf_ln_g, conf_ln_b, conf_w_pw2,
           conf_b_pw2, ffn_w_in, ffn_w_out):
    n, l, d = x.shape
    lctx = ctx.shape[1]

    mod_rows = 16
    cc = jnp.concatenate([c, c_ctx[None, :], jnp.zeros((mod_rows - n - 1, d), F32)], axis=0)
    mod = _ada_call(cc, ada_w, ada_b)

    def mods(i):
        lat = [mod[i, :n, k * d:(k + 1) * d][:, None, :] for k in range(6)]
        cx = [jnp.broadcast_to(mod[i, n:n + 1, k * d:(k + 1) * d][:, None, :], (n, 1, d)) for k in range(6)]
        return lat, cx

    def ffn(h, sh, sc, gt, i, tm, final_norm):
        wa = ffn_w_in[i, :, :FFN_HIDDEN].astype(BF16)
        wb = ffn_w_in[i, :, FFN_HIDDEN:].astype(BF16)
        wo = ffn_w_out[i].astype(BF16)
        return _ffn_call(h, sh, sc, gt, norm_ffn_g[i][None, :], wa, wb, wo, final_norm_g[None, :], tm,
                         final_norm)

    h_lat, h_ctx = x, ctx

    lat, cx = mods(0)
    w_in = ssd_w_in[0]
    pad = jnp.zeros((DT_LANES - 2 * HEADS,), F32)
    expand = lambda v: jnp.repeat(v, HEADDIM)[None, :]
    p = (
        w_in[:, :D_INNER].astype(BF16),
        w_in[:, D_INNER:D_INNER + CONV_DIM].astype(BF16),
        jnp.pad(w_in[:, D_INNER + CONV_DIM:], ((0, 0), (0, DT_LANES - 2 * HEADS))).astype(BF16),
        ssd_conv_w[0], ssd_conv_b[0],
        jnp.concatenate([ssd_dt_bias_f[0], ssd_dt_bias_b[0], pad])[None, :],
        jnp.concatenate([-jnp.exp(ssd_a_log_f[0]), -jnp.exp(ssd_a_log_b[0]), pad])[None, :],
        expand(ssd_d_skip[0]), ssd_norm_w[0][None, :], ssd_w_out[0].astype(BF16),
    )
    g_mix = norm_mix_g[0][None, :]
    h_zero = jnp.zeros((n, GROUPS, STATE, HPG * HEADDIM), F32)
    h_ctx, hf_ctx, hb_ctx = _ssd_mixer(h_ctx, cx[0], cx[1], cx[2], g_mix, h_zero, h_zero, p, lctx)
    h_lat, _, _ = _ssd_mixer(h_lat, lat[0], lat[1], lat[2], g_mix, hf_ctx, hb_ctx, p, 512)
    h_lat = ffn(h_lat, lat[3], lat[4], lat[5], 0, 512, False)
    h_ctx = ffn(h_ctx, cx[3], cx[4], cx[5], 0, lctx, False)

    lat, cx = mods(1)
    u = _conf_pw1_call(h_lat, lat[0], lat[1], norm_mix_g[1][None, :], conf_w_pw1[0].astype(BF16),
                       conf_b_pw1[0][None, :], 512)
    h_lat = _conf_conv_call(u, h_lat, lat[2], conf_dw_w[0], conf_dw_b[0], conf_ln_g[0], conf_ln_b[0],
                            conf_w_pw2[0].astype(BF16), conf_b_pw2[0])
    return ffn(h_lat, lat[3], lat[4], lat[5], 1, 512, True)
```

```python
import functools
import math

import jax
import jax.numpy as jnp
from jax import lax
from jax.experimental import pallas as pl
from jax.experimental.pallas import tpu as pltpu

F32 = jnp.float32
BF16 = jnp.bfloat16

EPS = 1e-6
LOG2E = math.log2(math.e)
D_MODEL = 1024
GRID_W = 64
D_INNER = 2048
HEADDIM = 64
HEADS = 32
GROUPS = 8
HPG = 4
STATE = 128
GN = GROUPS * STATE
CONV_K = 5
CONV_DIM = D_INNER + 2 * GN
DT_LANES = 128
SCAN_Q = 128
HALO = 16
CONF_K = 31
CONF_H = 512
CONF_ROWS = 16
SUBLANES = 8
FFN_HIDDEN = 2816
FFN_CHUNK = 256

VMEM_LIMIT = 56 * 1024 * 1024


def _const_spec(shape):
    nd = len(shape)
    return pl.BlockSpec(shape, lambda *_: (0,) * nd, pipeline_mode=pl.Buffered(1))


def _layer_spec(shape, layer):
    nd = len(shape)
    return pl.BlockSpec((None,) + tuple(shape[1:]), lambda *_: (layer,) + (0,) * (nd - 1),
                        pipeline_mode=pl.Buffered(1))


def _params(sem):
    return pltpu.CompilerParams(dimension_semantics=sem, vmem_limit_bytes=VMEM_LIMIT)


def _silu(x):
    return x * jax.nn.sigmoid(x)


def _softplus(x):
    return jnp.maximum(x, 0.0) + jnp.log1p(jnp.exp(-jnp.abs(x)))


def _norm_modulate(x, g, shift, scale):
    y = x * lax.rsqrt(jnp.mean(x * x, axis=-1, keepdims=True) + EPS) * g
    return y * (1.0 + scale) + shift


def _ada_kernel(c_ref, w_ref, b_ref, o_ref):
    s = _silu(c_ref[...]).astype(BF16)
    o_ref[...] = jnp.dot(s, w_ref[...].astype(BF16), preferred_element_type=F32) + b_ref[...]


def _ada_call(cc, ada_w, ada_b):
    depth, d, n6 = ada_w.shape
    tn = 1536
    rows = cc.shape[0]
    return pl.pallas_call(
        _ada_kernel,
        grid=(depth, n6 // tn),
        in_specs=[
            pl.BlockSpec((rows, d), lambda i, j: (0, 0)),
            pl.BlockSpec((None, d, tn), lambda i, j: (i, 0, j)),
            pl.BlockSpec((None, 1, tn), lambda i, j: (i, 0, j)),
        ],
        out_specs=pl.BlockSpec((None, rows, tn), lambda i, j: (i, 0, j)),
        out_shape=jax.ShapeDtypeStruct((depth, rows, n6), F32),
        compiler_params=_params(("arbitrary", "arbitrary")),
        name="ada_params",
    )(cc, ada_w, ada_b.reshape(depth, 1, n6))


def _inproj_kernel(x_ref, xp_ref, xn_ref, sh_ref, sc_ref, g_ref, w_ref, cw_ref, cb_ref,
                   z_ref, xbc_ref, dt_ref, xs_ref, xmp_ref, raw_ref, nat_ref):
    t = pl.program_id(1)
    nt = pl.num_programs(1)
    tm, d = x_ref.shape[1], x_ref.shape[2]
    rows = tm + 2 * HALO
    p = rows // SUBLANES
    lane = 128
    nm = lambda v: _norm_modulate(v, g_ref[...], sh_ref[0], sc_ref[0])

    def stage(lo, v):
        for c in range(d // lane):
            xs_ref[c, lo:lo + v.shape[0], :] = v[:, c * lane:(c + 1) * lane]

    nb = 512
    half = CONV_K // 2
    nblk = 4 if p % 4 == 0 else 2
    rb = rows // nblk

    main = nm(x_ref[0])
    xm = main.astype(BF16)
    ndt = w_ref.shape[1] - D_INNER - CONV_DIM
    dt_ref[0, :, :ndt] = jnp.dot(xm, w_ref[:, D_INNER + CONV_DIM:], preferred_element_type=F32)
    dt_ref[0, :, ndt:] = jnp.zeros((tm, DT_LANES - ndt), F32)
    for j in range(D_INNER // nb):
        zc = slice(j * nb, (j + 1) * nb)
        z_ref[0, :, zc] = jnp.dot(xm, w_ref[:, zc], preferred_element_type=F32).astype(BF16)

    stage(HALO, main)
    stage(0, nm(xp_ref[0, 0]) * jnp.where(t > 0, 1.0, 0.0))
    stage(HALO + tm, nm(xn_ref[0, 0]) * jnp.where(t < nt - 1, 1.0, 0.0))
    for c in range(d // lane):
        tiles = [xs_ref[c, pl.ds(i, SUBLANES, stride=p), :] for i in range(p)]
        xmp_ref[:, c * lane:(c + 1) * lane] = jnp.concatenate(tiles, axis=0).astype(BF16)

    for j in range(CONV_DIM // nb):
        cols = slice(j * nb, (j + 1) * nb)
        slot = j % 2
        raw = jnp.dot(xmp_ref[...], w_ref[:, D_INNER + j * nb:D_INNER + (j + 1) * nb],
                      preferred_element_type=F32)
        lead = half * SUBLANES
        for m in range(half):
            raw_ref[slot, m * SUBLANES:(m + 1) * SUBLANES, :] = pltpu.roll(
                raw[rows - lead + m * SUBLANES:rows - lead + (m + 1) * SUBLANES], 1, axis=0)
            raw_ref[slot, lead + rows + m * SUBLANES:lead + rows + (m + 1) * SUBLANES, :] = pltpu.roll(
                raw[m * SUBLANES:(m + 1) * SUBLANES], SUBLANES - 1, axis=0)
        raw_ref[slot, lead:lead + rows, :] = raw
        for r in range(nblk):
            for c in range(nb // lane):
                lc = slice(c * lane, (c + 1) * lane)
                wc = slice(j * nb + c * lane, j * nb + (c + 1) * lane)
                win = raw_ref[slot, r * rb:r * rb + rb + 2 * lead, lc]
                acc = cb_ref[:, wc]
                for k in range(CONV_K):
                    acc = acc + win[k * SUBLANES:k * SUBLANES + rb] * cw_ref[k:k + 1, wc]
                act = _silu(acc)
                for i in range(rb // SUBLANES):
                    nat_ref[slot, c, pl.ds(r * rb // SUBLANES + i, SUBLANES, stride=p), :] = (
                        act[i * SUBLANES:(i + 1) * SUBLANES])
        xbc_ref[0, :, cols] = jnp.concatenate(
            [nat_ref[slot, c, HALO:HALO + tm, :] for c in range(nb // lane)], axis=1).astype(BF16)


def _inproj_call(h, shift, scale, g, w, conv_w, conv_b, tm):
    n, l, d = h.shape
    hb = tm // HALO
    nhb = l // HALO
    h16 = h.reshape(n, nhb, HALO, d)
    row = lambda b, t: (b, t, 0)
    mod = lambda b, t: (b, 0, 0)
    return pl.pallas_call(
        _inproj_kernel,
        grid=(n, l // tm),
        in_specs=[
            pl.BlockSpec((1, tm, d), row),
            pl.BlockSpec((1, 1, HALO, d), lambda b, t: (b, jnp.maximum(t * hb - 1, 0), 0, 0)),
            pl.BlockSpec((1, 1, HALO, d), lambda b, t: (b, jnp.minimum((t + 1) * hb, nhb - 1), 0, 0)),
            pl.BlockSpec((1, 1, d), mod),
            pl.BlockSpec((1, 1, d), mod),
            _const_spec((1, d)),
            _layer_spec(w.shape, 0),
            _const_spec(conv_w.shape),
            _const_spec((1, CONV_DIM)),
        ],
        out_specs=[
            pl.BlockSpec((1, tm, D_INNER), row),
            pl.BlockSpec((1, tm, CONV_DIM), row),
            pl.BlockSpec((1, tm, DT_LANES), row),
        ],
        out_shape=[
            jax.ShapeDtypeStruct((n, l, D_INNER), BF16),
            jax.ShapeDtypeStruct((n, l, CONV_DIM), BF16),
            jax.ShapeDtypeStruct((n, l, DT_LANES), F32),
        ],
        scratch_shapes=[
            pltpu.VMEM((d // 128, tm + 2 * HALO, 128), F32),
            pltpu.VMEM((tm + 2 * HALO, d), BF16),
            pltpu.VMEM((2, tm + 2 * HALO + 2 * (CONV_K // 2) * SUBLANES, 512), F32),
            pltpu.VMEM((2, 4, tm + 2 * HALO, 128), F32),
        ],
        compiler_params=_params(("parallel", "arbitrary")),
        name="ssd_inproj_conv",
    )(h, h16, h16, shift, scale, g, w, conv_w, conv_b.reshape(1, CONV_DIM))


def _exact_cumsum_dot(tri, a):
    a1 = a.astype(BF16)
    r1 = a - a1.astype(F32)
    a2 = r1.astype(BF16)
    a3 = (r1 - a2.astype(F32)).astype(BF16)
    dot = lambda t: jnp.dot(tri, t, preferred_element_type=F32)
    return dot(a1) + dot(a2) + dot(a3)


def _decay_kernel(dt_ref, bias_ref, a_ref, ac_ref, rowt_ref, wt_ref):
    q = SCAN_Q
    rowi = lax.broadcasted_iota(jnp.int32, (q, q), 0)
    coli = lax.broadcasted_iota(jnp.int32, (q, q), 1)
    fwd = coli < HEADS
    tri_f = jnp.where(rowi >= coli, 1.0, 0.0).astype(BF16)
    tri_b = jnp.where(rowi <= coli, 1.0, 0.0).astype(BF16)
    for c in range(dt_ref.shape[1] // q):
        sp = _softplus(dt_ref[0, c * q:(c + 1) * q, :] + bias_ref[...])
        a = sp * a_ref[...]
        ac = jnp.where(fwd, _exact_cumsum_dot(tri_f, a), _exact_cumsum_dot(tri_b, a)) * LOG2E
        d = jnp.log2(sp) - ac
        total = jnp.where(fwd, ac[q - 1:q, :], ac[0:1, :])
        ac_ref[0, c * q:(c + 1) * q, :] = ac
        rowt_ref[0, c] = (-d).T
        wt_ref[0, c] = jnp.exp2(total + d).T


def _decay_call(dt, bias, a):
    n, l, _ = dt.shape
    nc = l // SCAN_Q
    cps = math.gcd(nc, 8)
    row = lambda b, c: (b, c, 0)
    tab = lambda b, c: (b, c, 0, 0)
    return pl.pallas_call(
        _decay_kernel,
        grid=(n, nc // cps),
        in_specs=[
            pl.BlockSpec((1, cps * SCAN_Q, DT_LANES), row),
            _const_spec((1, DT_LANES)),
            _const_spec((1, DT_LANES)),
        ],
        out_specs=[
            pl.BlockSpec((1, cps * SCAN_Q, DT_LANES), row),
            pl.BlockSpec((1, cps, DT_LANES, SCAN_Q), tab),
            pl.BlockSpec((1, cps, DT_LANES, SCAN_Q), tab),
        ],
        out_shape=[
            jax.ShapeDtypeStruct((n, l, DT_LANES), F32),
            jax.ShapeDtypeStruct((n, nc, DT_LANES, SCAN_Q), F32),
            jax.ShapeDtypeStruct((n, nc, DT_LANES, SCAN_Q), F32),
        ],
        compiler_params=_params(("parallel", "parallel")),
        name="ssd_decay_tables",
    )(dt, bias, a)


def _scan_step(xbc_ref, ac_ref, rowt_ref, wt_ref, state_ref, sc_ref, bt_ref, yst_ref,
               emit, *, reverse):
    q = SCAN_Q
    lane0 = HEADS if reverse else 0
    rowi = lax.broadcasted_iota(jnp.int32, (q, q), 0)
    coli = lax.broadcasted_iota(jnp.int32, (q, q), 1)
    causal = (rowi <= coli) if reverse else (rowi >= coli)
    first_head = coli < HEADDIM
    last = 0 if reverse else q - 1
    for g in range(GROUPS):
        b_g = xbc_ref[0, :, D_INNER + g * STATE:D_INNER + (g + 1) * STATE]
        c_g = xbc_ref[0, :, D_INNER + GN + g * STATE:D_INNER + GN + (g + 1) * STATE]
        sc_ref[g] = lax.dot_general(c_g, b_g, (((1,), (1,)), ((), ())), preferred_element_type=F32)
        bt_ref[g] = b_g.astype(F32).T
        yst_ref[:, g * HPG * HEADDIM:(g + 1) * HPG * HEADDIM] = jnp.dot(
            c_g, state_ref[g].astype(BF16), preferred_element_type=F32)
    for g in range(GROUPS):
        scores = sc_ref[g]
        b_gt = bt_ref[g]
        for pr in range(HPG // 2):
            lanes = slice((g * HPG + 2 * pr) * HEADDIM, (g * HPG + 2 * pr + 2) * HEADDIM)
            st_lanes = slice(2 * pr * HEADDIM, (2 * pr + 2) * HEADDIM)
            xs_pair = xbc_ref[0, :, lanes]
            state = state_ref[g, :, st_lanes]
            y_state = yst_ref[:, lanes]
            cols, m_parts, bw_parts, xs_parts = [], [], [], []
            for j in range(2):
                hl = lane0 + g * HPG + 2 * pr + j
                col_a = jnp.broadcast_to(ac_ref[0, :, hl:hl + 1], (q, q))
                seg = col_a - rowt_ref[0, 0, hl:hl + 1, :]
                m_parts.append((scores * jnp.where(causal, jnp.exp2(seg), 0.0)).astype(BF16))
                bw_parts.append((b_gt * wt_ref[0, 0, hl:hl + 1, :]).astype(BF16))
                keep = first_head if j == 0 else jnp.logical_not(first_head)
                xs_parts.append(jnp.where(keep, xs_pair, jnp.zeros_like(xs_pair)))
                cols.append(col_a)
            lhs = jnp.concatenate([jnp.concatenate(m_parts, axis=1), jnp.concatenate(bw_parts, axis=1)],
                                  axis=0)
            both = jnp.dot(lhs, jnp.concatenate(xs_parts, axis=0), preferred_element_type=F32)
            y_diag, upd = both[:q], both[q:]
            e_col = jnp.exp2(jnp.where(first_head, cols[0], cols[1]))
            emit(lanes, y_diag + e_col * y_state, xs_pair)
            cd = jnp.broadcast_to(e_col[last:last + 1, :], (STATE, q))
            state_ref[g, :, st_lanes] = state * cd + upd


def _scan_kernel(xbc_ref, ac_ref, rowt_ref, wt_ref, h0_ref, *rest, reverse):
    if reverse:
        y_ref, hfin_ref, state_ref, sc_ref, bt_ref, yst_ref = rest
    else:
        dskip_ref, y_ref, hfin_ref, state_ref, sc_ref, bt_ref, yst_ref = rest
    c = pl.program_id(1)

    @pl.when(c == 0)
    def _():
        state_ref[...] = h0_ref[0]

    def emit(lanes, y, xs):
        if not reverse:
            y = y + dskip_ref[:, lanes] * xs.astype(F32)
        y_ref[0, :, lanes] = y.astype(y_ref.dtype)

    _scan_step(xbc_ref, ac_ref, rowt_ref, wt_ref, state_ref, sc_ref, bt_ref, yst_ref,
               emit, reverse=reverse)

    @pl.when(c == pl.num_programs(1) - 1)
    def _():
        hfin_ref[0] = state_ref[...]


def _mixer_out_tile(yf_ref, yb_ref, z_ref, res_ref, gate_ref, nw_ref, wo_ref, acc_ref):
    kb = 512
    ssq = None
    for k in range(D_INNER // kb):
        cols = slice(k * kb, (k + 1) * kb)
        y = yf_ref[0, :, cols].astype(F32) + yb_ref[0, :, cols].astype(F32)
        yz = y * _silu(z_ref[0, :, cols].astype(F32))
        part = jnp.sum(yz * yz, axis=-1, keepdims=True)
        ssq = part if ssq is None else ssq + part
        prod = jnp.dot((yz * nw_ref[:, cols]).astype(BF16), wo_ref[cols, :], preferred_element_type=F32)
        if k == 0:
            acc_ref[...] = prod
        else:
            acc_ref[...] += prod
    out = acc_ref[...] * lax.rsqrt(ssq * (1.0 / D_INNER) + EPS)
    return res_ref[0] + gate_ref[0] * out


_STATE_BLOCK = (1, GROUPS, STATE, HPG * HEADDIM)


def _scan_scratch():
    return [
        pltpu.VMEM(_STATE_BLOCK[1:], F32),
        pltpu.VMEM((GROUPS, SCAN_Q, SCAN_Q), F32),
        pltpu.VMEM((GROUPS, STATE, SCAN_Q), F32),
        pltpu.VMEM((SCAN_Q, D_INNER), F32),
    ]


def _scan_call(xbc, ac, rowt, wt, h0, dskip=None):
    reverse = dskip is None
    n, l, _ = xbc.shape
    nc = l // SCAN_Q
    chunk = (lambda c: nc - 1 - c) if reverse else (lambda c: c)
    row = lambda b, c: (b, chunk(c), 0)
    tab = lambda b, c: (b, chunk(c), 0, 0)
    st = lambda b, c: (b, 0, 0, 0)
    extra_specs, extra_args = ([], ()) if reverse else ([_const_spec((1, D_INNER))], (dskip,))
    return pl.pallas_call(
        functools.partial(_scan_kernel, reverse=reverse),
        grid=(n, nc),
        in_specs=[
            pl.BlockSpec((1, SCAN_Q, CONV_DIM), row),
            pl.BlockSpec((1, SCAN_Q, DT_LANES), row),
            pl.BlockSpec((1, 1, DT_LANES, SCAN_Q), tab),
            pl.BlockSpec((1, 1, DT_LANES, SCAN_Q), tab),
            pl.BlockSpec(_STATE_BLOCK, st),
        ] + extra_specs,
        out_specs=[
            pl.BlockSpec((1, SCAN_Q, D_INNER), row),
            pl.BlockSpec(_STATE_BLOCK, st),
        ],
        out_shape=[
            jax.ShapeDtypeStruct((n, l, D_INNER), BF16),
            jax.ShapeDtypeStruct((n,) + _STATE_BLOCK[1:], F32),
        ],
        scratch_shapes=_scan_scratch(),
        compiler_params=_params(("parallel", "arbitrary")),
        name="ssd_scan_bwd" if reverse else "ssd_scan_fwd",
    )(xbc, ac, rowt, wt, h0, *extra_args)


def _ffn_tile(x, sh_ref, sc_ref, gt_ref, g_ref, wi_ref, wo_ref, gf_ref, acc_ref, final_norm):
    xm = _norm_modulate(x, g_ref[...], sh_ref[0], sc_ref[0]).astype(BF16)
    for j in range(FFN_HIDDEN // FFN_CHUNK):
        sl = slice(j * FFN_CHUNK, (j + 1) * FFN_CHUNK)
        ua = jnp.dot(xm, wi_ref[:, sl], preferred_element_type=F32)
        ub = jnp.dot(xm, wi_ref[:, FFN_HIDDEN + j * FFN_CHUNK:FFN_HIDDEN + (j + 1) * FFN_CHUNK],
                     preferred_element_type=F32)
        act = (_silu(ua) * ub).astype(BF16)
        part = jnp.dot(act, wo_ref[sl, :], preferred_element_type=F32)
        if j == 0:
            acc_ref[...] = part
        else:
            acc_ref[...] += part
    y = x + gt_ref[0] * acc_ref[...]
    if final_norm:
        y = y * lax.rsqrt(jnp.mean(y * y, axis=-1, keepdims=True) + EPS) * gf_ref[...]
    return y


def _ffn_kernel(x_ref, sh_ref, sc_ref, gt_ref, g_ref, wi_ref, wo_ref, gf_ref, o_ref, acc_ref,
                *, final_norm):
    o_ref[0] = _ffn_tile(x_ref[0], sh_ref, sc_ref, gt_ref, g_ref, wi_ref, wo_ref, gf_ref, acc_ref,
                         final_norm)


def _mixer_ffn_kernel(yf_ref, yb_ref, z_ref, res_ref, gm_ref, nw_ref, wm_ref, sh_ref, sc_ref, gt_ref,
                      g_ref, wi_ref, wo_ref, gf_ref, o_ref, macc_ref, h_ref, acc_ref):
    h_ref[...] = _mixer_out_tile(yf_ref, yb_ref, z_ref, res_ref, gm_ref, nw_ref, wm_ref, macc_ref)
    o_ref[0] = _ffn_tile(h_ref[...], sh_ref, sc_ref, gt_ref, g_ref, wi_ref, wo_ref, gf_ref, acc_ref,
                         False)


def _ffn_specs(d, wi, wo, layer):
    mod = pl.BlockSpec((1, 1, d), lambda b, t: (b, 0, 0))
    return [mod, mod, mod, _const_spec((1, d)), _layer_spec(wi.shape, layer), _layer_spec(wo.shape, layer),
            _const_spec((1, d))]


def _ffn_call(h, shift, scale, gate, g, wi, wo, layer, gf, tm, final_norm):
    n, l, d = h.shape
    row = pl.BlockSpec((1, tm, d), lambda b, t: (b, t, 0))
    return pl.pallas_call(
        functools.partial(_ffn_kernel, final_norm=final_norm),
        grid=(n, l // tm),
        in_specs=[row] + _ffn_specs(d, wi, wo, layer),
        out_specs=row,
        out_shape=jax.ShapeDtypeStruct((n, l, d), F32),
        scratch_shapes=[pltpu.VMEM((tm, d), F32)],
        compiler_params=_params(("parallel", "parallel")),
        name="swiglu_ffn",
    )(h, shift, scale, gate, g, wi, wo, gf)


def _mixer_ffn_call(yf, yb, z, res, gate_mix, nw, w_mix, shift, scale, gate, g, wi, wo, layer, gf, tm):
    n, l, d = res.shape
    row = pl.BlockSpec((1, tm, d), lambda b, t: (b, t, 0))
    wide = pl.BlockSpec((1, tm, D_INNER), lambda b, t: (b, t, 0))
    return pl.pallas_call(
        _mixer_ffn_kernel,
        grid=(n, l // tm),
        in_specs=[wide, wide, wide, row, pl.BlockSpec((1, 1, d), lambda b, t: (b, 0, 0)),
                  _const_spec((1, D_INNER)), _layer_spec(w_mix.shape, 0)] + _ffn_specs(d, wi, wo, layer),
        out_specs=row,
        out_shape=jax.ShapeDtypeStruct((n, l, d), F32),
        scratch_shapes=[pltpu.VMEM((tm, d), F32)] * 3,
        compiler_params=_params(("parallel", "parallel")),
        name="mixer_out_ffn",
    )(yf, yb, z, res, gate_mix, nw, w_mix, shift, scale, gate, g, wi, wo, gf)


def _conf_pw1_kernel(x_ref, sh_ref, sc_ref, g_ref, w_ref, b_ref, o_ref):
    xm = _norm_modulate(x_ref[0], g_ref[...], sh_ref[0], sc_ref[0]).astype(BF16)
    d = o_ref.shape[-1]
    ua = jnp.dot(xm, w_ref[:, :d], preferred_element_type=F32) + b_ref[:, :d]
    ub = jnp.dot(xm, w_ref[:, d:], preferred_element_type=F32) + b_ref[:, d:]
    o_ref[0] = ua * jax.nn.sigmoid(ub)


def _conf_pw1_call(h, shift, scale, g, w, b, tm):
    n, l, d = h.shape
    row = lambda bb, t: (bb, t, 0)
    mod = lambda bb, t: (bb, 0, 0)
    return pl.pallas_call(
        _conf_pw1_kernel,
        grid=(n, l // tm),
        in_specs=[
            pl.BlockSpec((1, tm, d), row),
            pl.BlockSpec((1, 1, d), mod),
            pl.BlockSpec((1, 1, d), mod),
            _const_spec((1, d)),
            _layer_spec(w.shape, 0),
            _const_spec(b.shape),
        ],
        out_specs=pl.BlockSpec((1, tm, d), row),
        out_shape=jax.ShapeDtypeStruct((n, l, d), F32),
        compiler_params=_params(("parallel", "parallel")),
        name="conf_pw1_glu",
    )(h, shift, scale, g, w, b)


_HPAD = 16
_HSPAN = GRID_W + 2 * _HPAD - SUBLANES


def _conf_conv_kernel(u_ref, up_ref, un_ref, res_ref, gate_ref, dw_ref, db_ref, lg_ref, lb_ref, w2_ref,
                      b2_ref, o_ref, hpad_ref, phase_ref, vbuf_ref, conv_ref):
    r = pl.program_id(1)
    nr = pl.num_programs(1)
    rr, half = CONF_ROWS, CONF_K // 2
    ch = CONF_H

    hpad_ref[:, 0:_HPAD, :] = jnp.zeros((rr, _HPAD, ch), F32)
    hpad_ref[:, _HPAD + GRID_W:, :] = jnp.zeros((rr, _HPAD, ch), F32)
    hpad_ref[:, _HPAD:_HPAD + GRID_W, :] = u_ref[0, :, :, :ch]
    vbuf_ref[half:half + rr] = u_ref[0, :, :, ch:]

    @pl.when(r > 0)
    def _():
        vbuf_ref[0:half] = up_ref[0, rr - half:, :, :]

    @pl.when(r == 0)
    def _():
        vbuf_ref[0:half] = jnp.zeros((half, GRID_W, ch), F32)

    @pl.when(r < nr - 1)
    def _():
        vbuf_ref[half + rr:] = un_ref[0, :half, :, :]

    @pl.when(r == nr - 1)
    def _():
        vbuf_ref[half + rr:] = jnp.zeros((half, GRID_W, ch), F32)

    def row_body(i, carry):
        for ph in range(1, SUBLANES):
            phase_ref[ph] = hpad_ref[i, ph:ph + _HSPAN, :]
        acc_h = jnp.broadcast_to(db_ref[:, :ch], (GRID_W, ch))
        acc_v = jnp.broadcast_to(db_ref[:, ch:], (GRID_W, ch))
        for k in range(CONF_K):
            s = _HPAD - half + k
            ph, al = s % SUBLANES, s - s % SUBLANES
            tap = hpad_ref[i, al:al + GRID_W, :] if ph == 0 else phase_ref[ph, al:al + GRID_W, :]
            acc_h = acc_h + tap * dw_ref[k:k + 1, :ch]
            acc_v = acc_v + vbuf_ref[i + k] * dw_ref[k:k + 1, ch:]
        base = pl.multiple_of(i * GRID_W, GRID_W)
        conv_ref[pl.ds(base, GRID_W), :ch] = acc_h
        conv_ref[pl.ds(base, GRID_W), ch:] = acc_v
        return carry

    lax.fori_loop(0, rr, row_body, 0)

    tr = 256
    for c in range(rr * GRID_W // tr):
        v = conv_ref[c * tr:(c + 1) * tr, :]
        mu = jnp.mean(v, axis=-1, keepdims=True)
        vc = v - mu
        var = jnp.mean(vc * vc, axis=-1, keepdims=True)
        y = _silu(vc * lax.rsqrt(var + EPS) * lg_ref[...] + lb_ref[...]).astype(BF16)
        out = jnp.dot(y, w2_ref[...], preferred_element_type=F32) + b2_ref[...]
        o_ref[0, c * tr:(c + 1) * tr, :] = res_ref[0, c * tr:(c + 1) * tr, :] + gate_ref[0] * out


def _conf_conv_call(u, res, gate, dw, db, lg, lb, w2, b2):
    n, l, c = u.shape
    rows = l // GRID_W
    nrb = rows // CONF_ROWS
    tok = CONF_ROWS * GRID_W
    u4 = u.reshape(n, rows, GRID_W, c)
    blk = (1, CONF_ROWS, GRID_W, c)
    hblk = (1, CONF_ROWS, GRID_W, CONF_H)
    flat = pl.BlockSpec((1, tok, c), lambda b, r: (b, r, 0))
    return pl.pallas_call(
        _conf_conv_kernel,
        grid=(n, nrb),
        in_specs=[
            pl.BlockSpec(blk, lambda b, r: (b, r, 0, 0)),
            pl.BlockSpec(hblk, lambda b, r: (b, jnp.maximum(r - 1, 0), 0, 1)),
            pl.BlockSpec(hblk, lambda b, r: (b, jnp.minimum(r + 1, nrb - 1), 0, 1)),
            flat,
            pl.BlockSpec((1, 1, c), lambda b, r: (b, 0, 0)),
            _const_spec(dw.shape),
            _const_spec((1, c)),
            _const_spec((1, c)),
            _const_spec((1, c)),
            _layer_spec(w2.shape, 0),
            _const_spec((1, c)),
        ],
        out_specs=flat,
        out_shape=jax.ShapeDtypeStruct((n, l, c), F32),
        scratch_shapes=[
            pltpu.VMEM((CONF_ROWS, GRID_W + 2 * _HPAD, CONF_H), F32),
            pltpu.VMEM((SUBLANES, _HSPAN, CONF_H), F32),
            pltpu.VMEM((CONF_ROWS + 2 * (CONF_K // 2), GRID_W, CONF_H), F32),
            pltpu.VMEM((tok, c), F32),
        ],
        compiler_params=_params(("parallel", "parallel")),
        name="conf_conv_ln_pw2",
    )(u4, u4, u4, res, gate, dw, db.reshape(1, c), lg.reshape(1, c), lb.reshape(1, c), w2,
      b2.reshape(1, c))


def _ssd_scans(h, shift, scale, g, h0_f, h0_b, p, tm):
    w_in, conv_w, conv_b, bias, a, dskip = p
    z, xbc, dt = _inproj_call(h, shift, scale, g, w_in, conv_w, conv_b, tm)
    ac, rowt, wt = _decay_call(dt, bias, a)
    yf, hf = _scan_call(xbc, ac, rowt, wt, h0_f, dskip)
    yb, hb = _scan_call(xbc, ac, rowt, wt, h0_b)
    return yf, yb, z, hf, hb


def kernel(x, c, ctx, c_ctx, ada_w, ada_b, norm_mix_g, norm_ffn_g, final_norm_g, ssd_w_in, ssd_conv_w,
           ssd_conv_b, ssd_dt_bias_f, ssd_dt_bias_b, ssd_a_log_f, ssd_a_log_b, ssd_d_skip, ssd_norm_w,
           ssd_w_out, conf_w_pw1, conf_b_pw1, conf_dw_w, conf_dw_b, conf_ln_g, conf_ln_b, conf_w_pw2,
           conf_b_pw2, ffn_w_in, ffn_w_out):
    n, l, d = x.shape
    lctx = ctx.shape[1]

    mod_rows = 16
    cc = jnp.concatenate([c, c_ctx[None, :], jnp.zeros((mod_rows - n - 1, d), F32)], axis=0)
    mod = _ada_call(cc, ada_w, ada_b)

    def mods(i):
        lat = [mod[i, :n, k * d:(k + 1) * d][:, None, :] for k in range(6)]
        cx = [jnp.broadcast_to(mod[i, n:n + 1, k * d:(k + 1) * d][:, None, :], (n, 1, d)) for k in range(6)]
        return lat, cx

    ffn_wi, ffn_wo = ffn_w_in.astype(BF16), ffn_w_out.astype(BF16)

    def ffn(h, sh, sc, gt, i, tm, final_norm):
        return _ffn_call(h, sh, sc, gt, norm_ffn_g[i][None, :], ffn_wi, ffn_wo, i, final_norm_g[None, :],
                         tm, final_norm)

    h_lat, h_ctx = x, ctx

    lat, cx = mods(0)
    pad = jnp.zeros((DT_LANES - 2 * HEADS,), F32)
    expand = lambda v: jnp.repeat(v, HEADDIM)[None, :]
    p = (
        ssd_w_in.astype(BF16), ssd_conv_w[0], ssd_conv_b[0],
        jnp.concatenate([ssd_dt_bias_f[0], ssd_dt_bias_b[0], pad])[None, :],
        jnp.concatenate([-jnp.exp(ssd_a_log_f[0]), -jnp.exp(ssd_a_log_b[0]), pad])[None, :],
        expand(ssd_d_skip[0]),
    )
    g_mix = norm_mix_g[0][None, :]
    w_mix_out = ssd_w_out.astype(BF16)

    def mixer_ffn(y_parts, h, m, tm):
        yf, yb, z = y_parts
        return _mixer_ffn_call(yf, yb, z, h, m[2], ssd_norm_w[0][None, :], w_mix_out, m[3], m[4], m[5],
                               norm_ffn_g[0][None, :], ffn_wi, ffn_wo, 0, final_norm_g[None, :], tm)

    h_zero = jnp.zeros((n, GROUPS, STATE, HPG * HEADDIM), F32)
    *y_ctx, hf_ctx, hb_ctx = _ssd_scans(h_ctx, cx[0], cx[1], g_mix, h_zero, h_zero, p, lctx)
    *y_lat, _, _ = _ssd_scans(h_lat, lat[0], lat[1], g_mix, hf_ctx, hb_ctx, p, 512)
    h_lat = mixer_ffn(y_lat, h_lat, lat, 512)
    h_ctx = mixer_ffn(y_ctx, h_ctx, cx, lctx)

    lat, cx = mods(1)
    u = _conf_pw1_call(h_lat, lat[0], lat[1], norm_mix_g[1][None, :], conf_w_pw1.astype(BF16),
                       conf_b_pw1[0][None, :], 512)
    h_lat = _conf_conv_call(u, h_lat, lat[2], conf_dw_w[0], conf_dw_b[0], conf_ln_g[0], conf_ln_b[0],
                            conf_w_pw2.astype(BF16), conf_b_pw2[0])
    return ffn(h_lat, lat[3], lat[4], lat[5], 1, 512, True)
```

```python
import functools
import math

import jax
import jax.numpy as jnp
from jax import lax
from jax.experimental import pallas as pl
from jax.experimental.pallas import tpu as pltpu

F32 = jnp.float32
BF16 = jnp.bfloat16

EPS = 1e-6
LOG2E = math.log2(math.e)
D_MODEL = 1024
GRID_W = 64
D_INNER = 2048
HEADDIM = 64
HEADS = 32
GROUPS = 8
HPG = 4
STATE = 128
GN = GROUPS * STATE
CONV_K = 5
CONV_DIM = D_INNER + 2 * GN
DT_LANES = 128
SCAN_Q = 128
HALO = 16
CONF_K = 31
CONF_H = 512
CONF_ROWS = 16
SUBLANES = 8
FFN_HIDDEN = 2816
FFN_CHUNK = 256

VMEM_LIMIT = 56 * 1024 * 1024


def _const_spec(shape):
    nd = len(shape)
    return pl.BlockSpec(shape, lambda *_: (0,) * nd, pipeline_mode=pl.Buffered(1))


def _layer_spec(shape, layer):
    nd = len(shape)
    return pl.BlockSpec((None,) + tuple(shape[1:]), lambda *_: (layer,) + (0,) * (nd - 1),
                        pipeline_mode=pl.Buffered(1))


def _params(sem):
    return pltpu.CompilerParams(dimension_semantics=sem, vmem_limit_bytes=VMEM_LIMIT)


def _silu(x):
    return x * jax.nn.sigmoid(x)


def _softplus(x):
    return jnp.maximum(x, 0.0) + jnp.log1p(jnp.exp(-jnp.abs(x)))


def _norm_modulate(x, g, shift, scale):
    y = x * lax.rsqrt(jnp.mean(x * x, axis=-1, keepdims=True) + EPS) * g
    return y * (1.0 + scale) + shift


def _ada_kernel(c_ref, w_ref, b_ref, o_ref):
    s = _silu(c_ref[...]).astype(BF16)
    o_ref[...] = jnp.dot(s, w_ref[...].astype(BF16), preferred_element_type=F32) + b_ref[...]


def _ada_call(cc, ada_w, ada_b):
    depth, d, n6 = ada_w.shape
    tn = 1536
    rows = cc.shape[0]
    return pl.pallas_call(
        _ada_kernel,
        grid=(depth, n6 // tn),
        in_specs=[
            pl.BlockSpec((rows, d), lambda i, j: (0, 0)),
            pl.BlockSpec((None, d, tn), lambda i, j: (i, 0, j)),
            pl.BlockSpec((None, 1, tn), lambda i, j: (i, 0, j)),
        ],
        out_specs=pl.BlockSpec((None, rows, tn), lambda i, j: (i, 0, j)),
        out_shape=jax.ShapeDtypeStruct((depth, rows, n6), F32),
        compiler_params=_params(("arbitrary", "arbitrary")),
        name="ada_params",
    )(cc, ada_w, ada_b.reshape(depth, 1, n6))


def _inproj_kernel(x_ref, xp_ref, xn_ref, sh_ref, sc_ref, g_ref, w_ref, cw_ref, cb_ref,
                   z_ref, xbc_ref, dt_ref, xs_ref, xmp_ref, raw_ref, nat_ref):
    t = pl.program_id(1)
    nt = pl.num_programs(1)
    tm, d = x_ref.shape[1], x_ref.shape[2]
    rows = tm + 2 * HALO
    p = rows // SUBLANES
    lane = 128
    nm = lambda v: _norm_modulate(v, g_ref[...], sh_ref[0], sc_ref[0])

    def stage(lo, v):
        for c in range(d // lane):
            xs_ref[c, lo:lo + v.shape[0], :] = v[:, c * lane:(c + 1) * lane]

    nb = 512
    half = CONV_K // 2
    nblk = 4 if p % 4 == 0 else 2
    rb = rows // nblk

    main = nm(x_ref[0])
    xm = main.astype(BF16)
    ndt = w_ref.shape[1] - D_INNER - CONV_DIM
    dt_ref[0, :, :ndt] = jnp.dot(xm, w_ref[:, D_INNER + CONV_DIM:], preferred_element_type=F32)
    dt_ref[0, :, ndt:] = jnp.zeros((tm, DT_LANES - ndt), F32)
    for j in range(D_INNER // nb):
        zc = slice(j * nb, (j + 1) * nb)
        z_ref[0, :, zc] = jnp.dot(xm, w_ref[:, zc], preferred_element_type=F32).astype(BF16)

    stage(HALO, main)
    stage(0, nm(xp_ref[0, 0]) * jnp.where(t > 0, 1.0, 0.0))
    stage(HALO + tm, nm(xn_ref[0, 0]) * jnp.where(t < nt - 1, 1.0, 0.0))
    for c in range(d // lane):
        tiles = [xs_ref[c, pl.ds(i, SUBLANES, stride=p), :] for i in range(p)]
        xmp_ref[:, c * lane:(c + 1) * lane] = jnp.concatenate(tiles, axis=0).astype(BF16)

    for j in range(CONV_DIM // nb):
        cols = slice(j * nb, (j + 1) * nb)
        slot = j % 2
        raw = jnp.dot(xmp_ref[...], w_ref[:, D_INNER + j * nb:D_INNER + (j + 1) * nb],
                      preferred_element_type=F32)
        lead = half * SUBLANES
        for m in range(half):
            raw_ref[slot, m * SUBLANES:(m + 1) * SUBLANES, :] = pltpu.roll(
                raw[rows - lead + m * SUBLANES:rows - lead + (m + 1) * SUBLANES], 1, axis=0)
            raw_ref[slot, lead + rows + m * SUBLANES:lead + rows + (m + 1) * SUBLANES, :] = pltpu.roll(
                raw[m * SUBLANES:(m + 1) * SUBLANES], SUBLANES - 1, axis=0)
        raw_ref[slot, lead:lead + rows, :] = raw
        for r in range(nblk):
            for c in range(nb // lane):
                lc = slice(c * lane, (c + 1) * lane)
                wc = slice(j * nb + c * lane, j * nb + (c + 1) * lane)
                win = raw_ref[slot, r * rb:r * rb + rb + 2 * lead, lc]
                acc = cb_ref[:, wc]
                for k in range(CONV_K):
                    acc = acc + win[k * SUBLANES:k * SUBLANES + rb] * cw_ref[k:k + 1, wc]
                act = _silu(acc)
                for i in range(rb // SUBLANES):
                    nat_ref[slot, c, pl.ds(r * rb // SUBLANES + i, SUBLANES, stride=p), :] = (
                        act[i * SUBLANES:(i + 1) * SUBLANES])
        xbc_ref[0, :, cols] = jnp.concatenate(
            [nat_ref[slot, c, HALO:HALO + tm, :] for c in range(nb // lane)], axis=1).astype(BF16)


def _inproj_call(h, shift, scale, g, w, conv_w, conv_b, tm):
    n, l, d = h.shape
    hb = tm // HALO
    nhb = l // HALO
    h16 = h.reshape(n, nhb, HALO, d)
    row = lambda b, t: (b, t, 0)
    mod = lambda b, t: (b, 0, 0)
    return pl.pallas_call(
        _inproj_kernel,
        grid=(n, l // tm),
        in_specs=[
            pl.BlockSpec((1, tm, d), row),
            pl.BlockSpec((1, 1, HALO, d), lambda b, t: (b, jnp.maximum(t * hb - 1, 0), 0, 0)),
            pl.BlockSpec((1, 1, HALO, d), lambda b, t: (b, jnp.minimum((t + 1) * hb, nhb - 1), 0, 0)),
            pl.BlockSpec((1, 1, d), mod),
            pl.BlockSpec((1, 1, d), mod),
            _const_spec((1, d)),
            _layer_spec(w.shape, 0),
            _const_spec(conv_w.shape),
            _const_spec((1, CONV_DIM)),
        ],
        out_specs=[
            pl.BlockSpec((1, tm, D_INNER), row),
            pl.BlockSpec((1, tm, CONV_DIM), row),
            pl.BlockSpec((1, tm, DT_LANES), row),
        ],
        out_shape=[
            jax.ShapeDtypeStruct((n, l, D_INNER), BF16),
            jax.ShapeDtypeStruct((n, l, CONV_DIM), BF16),
            jax.ShapeDtypeStruct((n, l, DT_LANES), F32),
        ],
        scratch_shapes=[
            pltpu.VMEM((d // 128, tm + 2 * HALO, 128), F32),
            pltpu.VMEM((tm + 2 * HALO, d), BF16),
            pltpu.VMEM((2, tm + 2 * HALO + 2 * (CONV_K // 2) * SUBLANES, 512), F32),
            pltpu.VMEM((2, 4, tm + 2 * HALO, 128), F32),
        ],
        compiler_params=_params(("parallel", "arbitrary")),
        name="ssd_inproj_conv",
    )(h, h16, h16, shift, scale, g, w, conv_w, conv_b.reshape(1, CONV_DIM))


def _exact_cumsum_dot(tri, a):
    a1 = a.astype(BF16)
    r1 = a - a1.astype(F32)
    a2 = r1.astype(BF16)
    a3 = (r1 - a2.astype(F32)).astype(BF16)
    dot = lambda t: jnp.dot(tri, t, preferred_element_type=F32)
    return dot(a1) + dot(a2) + dot(a3)


def _decay_kernel(dt_ref, bias_ref, a_ref, ac_ref, rowt_ref, wt_ref):
    q = SCAN_Q
    rowi = lax.broadcasted_iota(jnp.int32, (q, q), 0)
    coli = lax.broadcasted_iota(jnp.int32, (q, q), 1)
    fwd = coli < HEADS
    tri_f = jnp.where(rowi >= coli, 1.0, 0.0).astype(BF16)
    tri_b = jnp.where(rowi <= coli, 1.0, 0.0).astype(BF16)
    for c in range(dt_ref.shape[1] // q):
        sp = _softplus(dt_ref[0, c * q:(c + 1) * q, :] + bias_ref[...])
        a = sp * a_ref[...]
        ac = jnp.where(fwd, _exact_cumsum_dot(tri_f, a), _exact_cumsum_dot(tri_b, a)) * LOG2E
        d = jnp.log2(sp) - ac
        total = jnp.where(fwd, ac[q - 1:q, :], ac[0:1, :])
        ac_ref[0, c * q:(c + 1) * q, :] = ac
        rowt_ref[0, c] = (-d).T
        wt_ref[0, c] = jnp.exp2(total + d).T


def _decay_call(dt, bias, a):
    n, l, _ = dt.shape
    nc = l // SCAN_Q
    cps = math.gcd(nc, 8)
    row = lambda b, c: (b, c, 0)
    tab = lambda b, c: (b, c, 0, 0)
    return pl.pallas_call(
        _decay_kernel,
        grid=(n, nc // cps),
        in_specs=[
            pl.BlockSpec((1, cps * SCAN_Q, DT_LANES), row),
            _const_spec((1, DT_LANES)),
            _const_spec((1, DT_LANES)),
        ],
        out_specs=[
            pl.BlockSpec((1, cps * SCAN_Q, DT_LANES), row),
            pl.BlockSpec((1, cps, DT_LANES, SCAN_Q), tab),
            pl.BlockSpec((1, cps, DT_LANES, SCAN_Q), tab),
        ],
        out_shape=[
            jax.ShapeDtypeStruct((n, l, DT_LANES), F32),
            jax.ShapeDtypeStruct((n, nc, DT_LANES, SCAN_Q), F32),
            jax.ShapeDtypeStruct((n, nc, DT_LANES, SCAN_Q), F32),
        ],
        compiler_params=_params(("parallel", "parallel")),
        name="ssd_decay_tables",
    )(dt, bias, a)


def _scan_step(xbc_ref, ac_ref, rowt_ref, wt_ref, state_ref, sc_ref, bt_ref, yst_ref,
               emit, *, reverse):
    q = SCAN_Q
    lane0 = HEADS if reverse else 0
    rowi = lax.broadcasted_iota(jnp.int32, (q, q), 0)
    coli = lax.broadcasted_iota(jnp.int32, (q, q), 1)
    causal = (rowi <= coli) if reverse else (rowi >= coli)
    first_head = coli < HEADDIM
    last = 0 if reverse else q - 1
    for g in range(GROUPS):
        b_g = xbc_ref[0, :, D_INNER + g * STATE:D_INNER + (g + 1) * STATE]
        c_g = xbc_ref[0, :, D_INNER + GN + g * STATE:D_INNER + GN + (g + 1) * STATE]
        sc_ref[g] = lax.dot_general(c_g, b_g, (((1,), (1,)), ((), ())), preferred_element_type=F32)
        bt_ref[g] = b_g.astype(F32).T
        yst_ref[:, g * HPG * HEADDIM:(g + 1) * HPG * HEADDIM] = jnp.dot(
            c_g, state_ref[g].astype(BF16), preferred_element_type=F32)
    for g in range(GROUPS):
        scores = sc_ref[g]
        b_gt = bt_ref[g]
        for pr in range(HPG // 2):
            lanes = slice((g * HPG + 2 * pr) * HEADDIM, (g * HPG + 2 * pr + 2) * HEADDIM)
            st_lanes = slice(2 * pr * HEADDIM, (2 * pr + 2) * HEADDIM)
            xs_pair = xbc_ref[0, :, lanes]
            state = state_ref[g, :, st_lanes]
            y_state = yst_ref[:, lanes]
            cols, m_parts, bw_parts, xs_parts = [], [], [], []
            for j in range(2):
                hl = lane0 + g * HPG + 2 * pr + j
                col_a = jnp.broadcast_to(ac_ref[0, :, hl:hl + 1], (q, q))
                seg = col_a - rowt_ref[0, 0, hl:hl + 1, :]
                m_parts.append((scores * jnp.where(causal, jnp.exp2(seg), 0.0)).astype(BF16))
                bw_parts.append((b_gt * wt_ref[0, 0, hl:hl + 1, :]).astype(BF16))
                keep = first_head if j == 0 else jnp.logical_not(first_head)
                xs_parts.append(jnp.where(keep, xs_pair, jnp.zeros_like(xs_pair)))
                cols.append(col_a)
            lhs = jnp.concatenate([jnp.concatenate(m_parts, axis=1), jnp.concatenate(bw_parts, axis=1)],
                                  axis=0)
            both = jnp.dot(lhs, jnp.concatenate(xs_parts, axis=0), preferred_element_type=F32)
            y_diag, upd = both[:q], both[q:]
            e_col = jnp.exp2(jnp.where(first_head, cols[0], cols[1]))
            emit(lanes, y_diag + e_col * y_state, xs_pair)
            cd = jnp.broadcast_to(e_col[last:last + 1, :], (STATE, q))
            state_ref[g, :, st_lanes] = state * cd + upd


def _scan_kernel(xbc_ref, ac_ref, rowt_ref, wt_ref, h0_ref, *rest, reverse):
    if reverse:
        y_ref, hfin_ref, state_ref, sc_ref, bt_ref, yst_ref = rest
    else:
        dskip_ref, y_ref, hfin_ref, state_ref, sc_ref, bt_ref, yst_ref = rest
    c = pl.program_id(1)

    @pl.when(c == 0)
    def _():
        state_ref[...] = h0_ref[0]

    def emit(lanes, y, xs):
        if not reverse:
            y = y + dskip_ref[:, lanes] * xs.astype(F32)
        y_ref[0, :, lanes] = y.astype(y_ref.dtype)

    _scan_step(xbc_ref, ac_ref, rowt_ref, wt_ref, state_ref, sc_ref, bt_ref, yst_ref,
               emit, reverse=reverse)

    @pl.when(c == pl.num_programs(1) - 1)
    def _():
        hfin_ref[0] = state_ref[...]


def _mixer_out_tile(yf_ref, yb_ref, z_ref, res_ref, gate_ref, nw_ref, wo_ref, acc_ref):
    kb = 256
    ssq = None
    for k in range(D_INNER // kb):
        cols = slice(k * kb, (k + 1) * kb)
        y = yf_ref[0, :, cols].astype(F32) + yb_ref[0, :, cols].astype(F32)
        yz = y * _silu(z_ref[0, :, cols].astype(F32))
        part = jnp.sum(yz * yz, axis=-1, keepdims=True)
        ssq = part if ssq is None else ssq + part
        prod = jnp.dot((yz * nw_ref[:, cols]).astype(BF16), wo_ref[cols, :], preferred_element_type=F32)
        if k == 0:
            acc_ref[...] = prod
        else:
            acc_ref[...] += prod
    out = acc_ref[...] * lax.rsqrt(ssq * (1.0 / D_INNER) + EPS)
    return res_ref[0] + gate_ref[0] * out


_STATE_BLOCK = (1, GROUPS, STATE, HPG * HEADDIM)


def _scan_scratch():
    return [
        pltpu.VMEM(_STATE_BLOCK[1:], F32),
        pltpu.VMEM((GROUPS, SCAN_Q, SCAN_Q), F32),
        pltpu.VMEM((GROUPS, STATE, SCAN_Q), F32),
        pltpu.VMEM((SCAN_Q, D_INNER), F32),
    ]


def _scan_call(xbc, ac, rowt, wt, h0, dskip=None):
    reverse = dskip is None
    n, l, _ = xbc.shape
    nc = l // SCAN_Q
    chunk = (lambda c: nc - 1 - c) if reverse else (lambda c: c)
    row = lambda b, c: (b, chunk(c), 0)
    tab = lambda b, c: (b, chunk(c), 0, 0)
    st = lambda b, c: (b, 0, 0, 0)
    extra_specs, extra_args = ([], ()) if reverse else ([_const_spec((1, D_INNER))], (dskip,))
    return pl.pallas_call(
        functools.partial(_scan_kernel, reverse=reverse),
        grid=(n, nc),
        in_specs=[
            pl.BlockSpec((1, SCAN_Q, CONV_DIM), row),
            pl.BlockSpec((1, SCAN_Q, DT_LANES), row),
            pl.BlockSpec((1, 1, DT_LANES, SCAN_Q), tab),
            pl.BlockSpec((1, 1, DT_LANES, SCAN_Q), tab),
            pl.BlockSpec(_STATE_BLOCK, st),
        ] + extra_specs,
        out_specs=[
            pl.BlockSpec((1, SCAN_Q, D_INNER), row),
            pl.BlockSpec(_STATE_BLOCK, st),
        ],
        out_shape=[
            jax.ShapeDtypeStruct((n, l, D_INNER), BF16),
            jax.ShapeDtypeStruct((n,) + _STATE_BLOCK[1:], F32),
        ],
        scratch_shapes=_scan_scratch(),
        compiler_params=_params(("parallel", "arbitrary")),
        name="ssd_scan_bwd" if reverse else "ssd_scan_fwd",
    )(xbc, ac, rowt, wt, h0, *extra_args)


def _ffn_tile(x, sh_ref, sc_ref, gt_ref, g_ref, wi_ref, wo_ref, gf_ref, acc_ref, final_norm):
    xm = _norm_modulate(x, g_ref[...], sh_ref[0], sc_ref[0]).astype(BF16)
    for j in range(FFN_HIDDEN // FFN_CHUNK):
        sl = slice(j * FFN_CHUNK, (j + 1) * FFN_CHUNK)
        ua = jnp.dot(xm, wi_ref[:, sl], preferred_element_type=F32)
        ub = jnp.dot(xm, wi_ref[:, FFN_HIDDEN + j * FFN_CHUNK:FFN_HIDDEN + (j + 1) * FFN_CHUNK],
                     preferred_element_type=F32)
        act = (_silu(ua) * ub).astype(BF16)
        part = jnp.dot(act, wo_ref[sl, :], preferred_element_type=F32)
        if j == 0:
            acc_ref[...] = part
        else:
            acc_ref[...] += part
    y = x + gt_ref[0] * acc_ref[...]
    if final_norm:
        y = y * lax.rsqrt(jnp.mean(y * y, axis=-1, keepdims=True) + EPS) * gf_ref[...]
    return y


def _ffn_kernel(x_ref, sh_ref, sc_ref, gt_ref, g_ref, wi_ref, wo_ref, gf_ref, o_ref, acc_ref,
                *, final_norm):
    o_ref[0] = _ffn_tile(x_ref[0], sh_ref, sc_ref, gt_ref, g_ref, wi_ref, wo_ref, gf_ref, acc_ref,
                         final_norm)


def _mixer_ffn_kernel(yf_ref, yb_ref, z_ref, res_ref, gm_ref, nw_ref, wm_ref, sh_ref, sc_ref, gt_ref,
                      g_ref, wi_ref, wo_ref, gf_ref, o_ref, macc_ref, h_ref, acc_ref):
    h_ref[...] = _mixer_out_tile(yf_ref, yb_ref, z_ref, res_ref, gm_ref, nw_ref, wm_ref, macc_ref)
    o_ref[0] = _ffn_tile(h_ref[...], sh_ref, sc_ref, gt_ref, g_ref, wi_ref, wo_ref, gf_ref, acc_ref,
                         False)


def _ffn_specs(d, wi, wo, layer):
    mod = pl.BlockSpec((1, 1, d), lambda b, t: (b, 0, 0))
    return [mod, mod, mod, _const_spec((1, d)), _layer_spec(wi.shape, layer), _layer_spec(wo.shape, layer),
            _const_spec((1, d))]


def _ffn_call(h, shift, scale, gate, g, wi, wo, layer, gf, tm, final_norm):
    n, l, d = h.shape
    row = pl.BlockSpec((1, tm, d), lambda b, t: (b, t, 0))
    return pl.pallas_call(
        functools.partial(_ffn_kernel, final_norm=final_norm),
        grid=(n, l // tm),
        in_specs=[row] + _ffn_specs(d, wi, wo, layer),
        out_specs=row,
        out_shape=jax.ShapeDtypeStruct((n, l, d), F32),
        scratch_shapes=[pltpu.VMEM((tm, d), F32)],
        compiler_params=_params(("parallel", "parallel")),
        name="swiglu_ffn",
    )(h, shift, scale, gate, g, wi, wo, gf)


def _mixer_ffn_call(yf, yb, z, res, gate_mix, nw, w_mix, shift, scale, gate, g, wi, wo, layer, gf, tm):
    n, l, d = res.shape
    row = pl.BlockSpec((1, tm, d), lambda b, t: (b, t, 0))
    wide = pl.BlockSpec((1, tm, D_INNER), lambda b, t: (b, t, 0))
    return pl.pallas_call(
        _mixer_ffn_kernel,
        grid=(n, l // tm),
        in_specs=[wide, wide, wide, row, pl.BlockSpec((1, 1, d), lambda b, t: (b, 0, 0)),
                  _const_spec((1, D_INNER)), _layer_spec(w_mix.shape, 0)] + _ffn_specs(d, wi, wo, layer),
        out_specs=row,
        out_shape=jax.ShapeDtypeStruct((n, l, d), F32),
        scratch_shapes=[pltpu.VMEM((tm, d), F32)] * 3,
        compiler_params=_params(("parallel", "parallel")),
        name="mixer_out_ffn",
    )(yf, yb, z, res, gate_mix, nw, w_mix, shift, scale, gate, g, wi, wo, gf)


def _conf_pw1_kernel(x_ref, sh_ref, sc_ref, g_ref, w_ref, b_ref, o_ref):
    xm = _norm_modulate(x_ref[0], g_ref[...], sh_ref[0], sc_ref[0]).astype(BF16)
    d = o_ref.shape[-1]
    ua = jnp.dot(xm, w_ref[:, :d], preferred_element_type=F32) + b_ref[:, :d]
    ub = jnp.dot(xm, w_ref[:, d:], preferred_element_type=F32) + b_ref[:, d:]
    o_ref[0] = ua * jax.nn.sigmoid(ub)


def _conf_pw1_call(h, shift, scale, g, w, b, tm):
    n, l, d = h.shape
    row = lambda bb, t: (bb, t, 0)
    mod = lambda bb, t: (bb, 0, 0)
    return pl.pallas_call(
        _conf_pw1_kernel,
        grid=(n, l // tm),
        in_specs=[
            pl.BlockSpec((1, tm, d), row),
            pl.BlockSpec((1, 1, d), mod),
            pl.BlockSpec((1, 1, d), mod),
            _const_spec((1, d)),
            _layer_spec(w.shape, 0),
            _const_spec(b.shape),
        ],
        out_specs=pl.BlockSpec((1, tm, d), row),
        out_shape=jax.ShapeDtypeStruct((n, l, d), F32),
        compiler_params=_params(("parallel", "parallel")),
        name="conf_pw1_glu",
    )(h, shift, scale, g, w, b)


_HPAD = 16
_HSPAN = GRID_W + 2 * _HPAD - SUBLANES


def _conf_conv_kernel(u_ref, up_ref, un_ref, res_ref, gate_ref, dw_ref, db_ref, lg_ref, lb_ref, w2_ref,
                      b2_ref, o_ref, hpad_ref, phase_ref, vbuf_ref, conv_ref):
    r = pl.program_id(1)
    nr = pl.num_programs(1)
    rr, half = CONF_ROWS, CONF_K // 2
    ch = CONF_H

    hpad_ref[:, 0:_HPAD, :] = jnp.zeros((rr, _HPAD, ch), F32)
    hpad_ref[:, _HPAD + GRID_W:, :] = jnp.zeros((rr, _HPAD, ch), F32)
    hpad_ref[:, _HPAD:_HPAD + GRID_W, :] = u_ref[0, :, :, :ch]
    vbuf_ref[half:half + rr] = u_ref[0, :, :, ch:]

    @pl.when(r > 0)
    def _():
        vbuf_ref[0:half] = up_ref[0, rr - half:, :, :]

    @pl.when(r == 0)
    def _():
        vbuf_ref[0:half] = jnp.zeros((half, GRID_W, ch), F32)

    @pl.when(r < nr - 1)
    def _():
        vbuf_ref[half + rr:] = un_ref[0, :half, :, :]

    @pl.when(r == nr - 1)
    def _():
        vbuf_ref[half + rr:] = jnp.zeros((half, GRID_W, ch), F32)

    def row_body(i, carry):
        for ph in range(1, SUBLANES):
            phase_ref[ph] = hpad_ref[i, ph:ph + _HSPAN, :]
        acc_h = jnp.broadcast_to(db_ref[:, :ch], (GRID_W, ch))
        acc_v = jnp.broadcast_to(db_ref[:, ch:], (GRID_W, ch))
        for k in range(CONF_K):
            s = _HPAD - half + k
            ph, al = s % SUBLANES, s - s % SUBLANES
            tap = hpad_ref[i, al:al + GRID_W, :] if ph == 0 else phase_ref[ph, al:al + GRID_W, :]
            acc_h = acc_h + tap * dw_ref[k:k + 1, :ch]
            acc_v = acc_v + vbuf_ref[i + k] * dw_ref[k:k + 1, ch:]
        base = pl.multiple_of(i * GRID_W, GRID_W)
        conv_ref[pl.ds(base, GRID_W), :ch] = acc_h
        conv_ref[pl.ds(base, GRID_W), ch:] = acc_v
        return carry

    lax.fori_loop(0, rr, row_body, 0)

    tr = 256
    for c in range(rr * GRID_W // tr):
        v = conv_ref[c * tr:(c + 1) * tr, :]
        mu = jnp.mean(v, axis=-1, keepdims=True)
        vc = v - mu
        var = jnp.mean(vc * vc, axis=-1, keepdims=True)
        y = _silu(vc * lax.rsqrt(var + EPS) * lg_ref[...] + lb_ref[...]).astype(BF16)
        out = jnp.dot(y, w2_ref[...], preferred_element_type=F32) + b2_ref[...]
        o_ref[0, c * tr:(c + 1) * tr, :] = res_ref[0, c * tr:(c + 1) * tr, :] + gate_ref[0] * out


def _conf_conv_call(u, res, gate, dw, db, lg, lb, w2, b2):
    n, l, c = u.shape
    rows = l // GRID_W
    nrb = rows // CONF_ROWS
    tok = CONF_ROWS * GRID_W
    u4 = u.reshape(n, rows, GRID_W, c)
    blk = (1, CONF_ROWS, GRID_W, c)
    hblk = (1, CONF_ROWS, GRID_W, CONF_H)
    flat = pl.BlockSpec((1, tok, c), lambda b, r: (b, r, 0))
    return pl.pallas_call(
        _conf_conv_kernel,
        grid=(n, nrb),
        in_specs=[
            pl.BlockSpec(blk, lambda b, r: (b, r, 0, 0)),
            pl.BlockSpec(hblk, lambda b, r: (b, jnp.maximum(r - 1, 0), 0, 1)),
            pl.BlockSpec(hblk, lambda b, r: (b, jnp.minimum(r + 1, nrb - 1), 0, 1)),
            flat,
            pl.BlockSpec((1, 1, c), lambda b, r: (b, 0, 0)),
            _const_spec(dw.shape),
            _const_spec((1, c)),
            _const_spec((1, c)),
            _const_spec((1, c)),
            _layer_spec(w2.shape, 0),
            _const_spec((1, c)),
        ],
        out_specs=flat,
        out_shape=jax.ShapeDtypeStruct((n, l, c), F32),
        scratch_shapes=[
            pltpu.VMEM((CONF_ROWS, GRID_W + 2 * _HPAD, CONF_H), F32),
            pltpu.VMEM((SUBLANES, _HSPAN, CONF_H), F32),
            pltpu.VMEM((CONF_ROWS + 2 * (CONF_K // 2), GRID_W, CONF_H), F32),
            pltpu.VMEM((tok, c), F32),
        ],
        compiler_params=_params(("parallel", "parallel")),
        name="conf_conv_ln_pw2",
    )(u4, u4, u4, res, gate, dw, db.reshape(1, c), lg.reshape(1, c), lb.reshape(1, c), w2,
      b2.reshape(1, c))


def _ssd_scans(h, shift, scale, g, h0_f, h0_b, p, tm):
    w_in, conv_w, conv_b, bias, a, dskip = p
    z, xbc, dt = _inproj_call(h, shift, scale, g, w_in, conv_w, conv_b, tm)
    ac, rowt, wt = _decay_call(dt, bias, a)
    yf, hf = _scan_call(xbc, ac, rowt, wt, h0_f, dskip)
    yb, hb = _scan_call(xbc, ac, rowt, wt, h0_b)
    return yf, yb, z, hf, hb


def kernel(x, c, ctx, c_ctx, ada_w, ada_b, norm_mix_g, norm_ffn_g, final_norm_g, ssd_w_in, ssd_conv_w,
           ssd_conv_b, ssd_dt_bias_f, ssd_dt_bias_b, ssd_a_log_f, ssd_a_log_b, ssd_d_skip, ssd_norm_w,
           ssd_w_out, conf_w_pw1, conf_b_pw1, conf_dw_w, conf_dw_b, conf_ln_g, conf_ln_b, conf_w_pw2,
           conf_b_pw2, ffn_w_in, ffn_w_out):
    n, l, d = x.shape
    lctx = ctx.shape[1]

    mod_rows = 16
    cc = jnp.concatenate([c, c_ctx[None, :], jnp.zeros((mod_rows - n - 1, d), F32)], axis=0)
    mod = _ada_call(cc, ada_w, ada_b)

    def mods(i):
        lat = [mod[i, :n, k * d:(k + 1) * d][:, None, :] for k in range(6)]
        cx = [jnp.broadcast_to(mod[i, n:n + 1, k * d:(k + 1) * d][:, None, :], (n, 1, d)) for k in range(6)]
        return lat, cx

    ffn_wi, ffn_wo = ffn_w_in.astype(BF16), ffn_w_out.astype(BF16)

    def ffn(h, sh, sc, gt, i, tm, final_norm):
        return _ffn_call(h, sh, sc, gt, norm_ffn_g[i][None, :], ffn_wi, ffn_wo, i, final_norm_g[None, :],
                         tm, final_norm)

    h_lat, h_ctx = x, ctx

    lat, cx = mods(0)
    pad = jnp.zeros((DT_LANES - 2 * HEADS,), F32)
    expand = lambda v: jnp.repeat(v, HEADDIM)[None, :]
    p = (
        ssd_w_in.astype(BF16), ssd_conv_w[0], ssd_conv_b[0],
        jnp.concatenate([ssd_dt_bias_f[0], ssd_dt_bias_b[0], pad])[None, :],
        jnp.concatenate([-jnp.exp(ssd_a_log_f[0]), -jnp.exp(ssd_a_log_b[0]), pad])[None, :],
        expand(ssd_d_skip[0]),
    )
    g_mix = norm_mix_g[0][None, :]
    w_mix_out = ssd_w_out.astype(BF16)

    def mixer_ffn(y_parts, h, m, tm):
        yf, yb, z = y_parts
        return _mixer_ffn_call(yf, yb, z, h, m[2], ssd_norm_w[0][None, :], w_mix_out, m[3], m[4], m[5],
                               norm_ffn_g[0][None, :], ffn_wi, ffn_wo, 0, final_norm_g[None, :], tm)

    h_zero = jnp.zeros((n, GROUPS, STATE, HPG * HEADDIM), F32)
    *y_ctx, hf_ctx, hb_ctx = _ssd_scans(h_ctx, cx[0], cx[1], g_mix, h_zero, h_zero, p, lctx)
    *y_lat, _, _ = _ssd_scans(h_lat, lat[0], lat[1], g_mix, hf_ctx, hb_ctx, p, 512)
    h_lat = mixer_ffn(y_lat, h_lat, lat, 512)
    h_ctx = mixer_ffn(y_ctx, h_ctx, cx, lctx)

    lat, cx = mods(1)
    u = _conf_pw1_call(h_lat, lat[0], lat[1], norm_mix_g[1][None, :], conf_w_pw1.astype(BF16),
                       conf_b_pw1[0][None, :], 512)
    h_lat = _conf_conv_call(u, h_lat, lat[2], conf_dw_w[0], conf_dw_b[0], conf_ln_g[0], conf_ln_b[0],
                            conf_w_pw2.astype(BF16), conf_b_pw2[0])
    return ffn(h_lat, lat[3], lat[4], lat[5], 1, 512, True)
```
